```python
import math
import jax, jax.numpy as jnp
from jax import lax
import numpy as np

D_MODEL = 1024
BATCH = 8
SEQ = 4096
DEPTH = 2

SSD_D_INNER = D_MODEL
SSD_HEAD_DIM = 64
SSD_HEADS = SSD_D_INNER // SSD_HEAD_DIM
SSD_GROUPS = 2
SSD_D_STATE = 128
SSD_CONV = 4
SSD_CHUNK = 128
SSD_CONV_DIM = SSD_D_INNER + 2 * SSD_GROUPS * SSD_D_STATE
DA_HEADS = 8
DA_HEAD_DIM = 64
DA_V_DIM = 2 * DA_HEAD_DIM
DA_QK_WIDTH = DA_HEADS * 2 * DA_HEAD_DIM
DA_V_WIDTH = DA_HEADS * DA_V_DIM
Q_BLOCK = 128
ROPE_THETA = 10000.0
EVEN_SPLITS = [SSD_D_INNER,
               SSD_D_INNER + SSD_CONV_DIM,
               SSD_D_INNER + SSD_CONV_DIM + SSD_HEADS,
               SSD_D_INNER + SSD_CONV_DIM + SSD_HEADS + DA_QK_WIDTH,
               SSD_D_INNER + SSD_CONV_DIM + SSD_HEADS + 2 * DA_QK_WIDTH]
EVEN_IN_WIDTH = SSD_D_INNER + SSD_CONV_DIM + SSD_HEADS + 2 * DA_QK_WIDTH + DA_V_WIDTH
EVEN_OUT_WIDTH = SSD_D_INNER + DA_V_WIDTH
SC_WIDTH = D_MODEL
SC_CONV = 3
D_FF = 2816
FFN_CONV = 3
RMS_EPS = 1e-6
N_EVEN = (DEPTH + 1) // 2
N_ODD = DEPTH // 2

kernel_name = "hybrid_ssd_diffattn_shortconv_convffn"


def rms_norm(x, w):
    xf = x.astype(jnp.float32)
    xf = xf * lax.rsqrt(jnp.mean(xf * xf, axis=-1, keepdims=True) + RMS_EPS)
    return (xf * w.astype(jnp.float32)).astype(x.dtype)


def causal_dwconv(x, w, b=None):
    K, C = w.shape
    y = lax.conv_general_dilated(x, w[:, None, :].astype(x.dtype), window_strides=(1,),
                                 padding=[(K - 1, 0)],
                                 dimension_numbers=('NWC', 'WIO', 'NWC'),
                                 feature_group_count=C)
    if b is not None:
        y = y + b.astype(x.dtype)
    return y


def rope_tables(seq, dim, dtype):
    inv = 1.0 / (ROPE_THETA ** (jnp.arange(0, dim, 2, dtype=jnp.float32) / dim))
    ang = jnp.arange(seq, dtype=jnp.float32)[:, None] * inv[None, :]
    ang = jnp.concatenate([ang, ang], axis=-1)
    return jnp.cos(ang).astype(dtype), jnp.sin(ang).astype(dtype)


def apply_rope(x, cos, sin):
    x1, x2 = jnp.split(x, 2, axis=-1)
    rot = jnp.concatenate([-x2, x1], axis=-1)
    return x * cos[:, None, :] + rot * sin[:, None, :]


def ssd_chunked(x, dt, A, Bm, Cm):
    b, L, H, P = x.shape
    G, N = Bm.shape[2], Bm.shape[3]
    R = H // G
    c = L // SSD_CHUNK
    X = (x * dt[..., None]).reshape(b, c, SSD_CHUNK, G, R, P)
    Adt = (dt * A).reshape(b, c, SSD_CHUNK, G, R)
    Bc = Bm.reshape(b, c, SSD_CHUNK, G, N)
    Cc = Cm.reshape(b, c, SSD_CHUNK, G, N)
    A_cs = jnp.cumsum(Adt, axis=2)
    A_t = A_cs.transpose(0, 1, 3, 4, 2)
    seg = A_t[..., :, None] - A_t[..., None, :]
    tril = jnp.tril(jnp.ones((SSD_CHUNK, SSD_CHUNK), dtype=bool))
    Lmat = jnp.exp(jnp.where(tril, seg, -jnp.inf))
    CB = jnp.einsum('bclgn,bcsgn->bcgls', Cc, Bc)
    Y_diag = jnp.einsum('bcgrls,bcsgrp->bclgrp', CB[:, :, :, None] * Lmat, X)
    decay_states = jnp.exp(A_cs[:, :, -1:] - A_cs)
    states = jnp.einsum('bclgn,bclgrp->bcgrpn', Bc, X * decay_states[..., None])
    chunk_decay = jnp.exp(A_cs[:, :, -1])

    def step(carry, inp):
        st, dec = inp
        return carry * dec[..., None, None] + st, carry

    init = jnp.zeros((b, G, R, P, N), dtype=X.dtype)
    _, prev = lax.scan(step, init, (states.transpose(1, 0, 2, 3, 4, 5),
                                    chunk_decay.transpose(1, 0, 2, 3)))
    prev = prev.transpose(1, 0, 2, 3, 4, 5)
    Y_off = jnp.einsum('bclgn,bcgrpn->bclgrp', Cc, prev) * jnp.exp(A_cs)[..., None]
    return (Y_diag + Y_off).reshape(b, L, H, P)


def diff_attention(q, k, v, lam, lambda_init, subln_w):
    b, L = q.shape[0], q.shape[1]
    nblk = L // Q_BLOCK
    scale = DA_HEAD_DIM ** -0.5
    kh = k.transpose(0, 2, 1, 3)
    vh = v.transpose(0, 2, 1, 3)
    qblocks = q.transpose(0, 2, 1, 3).reshape(b, 2 * DA_HEADS, nblk, Q_BLOCK, DA_HEAD_DIM)
    qblocks = qblocks.transpose(2, 0, 1, 3, 4)
    kpos = jnp.arange(L)

    def block(args):
        qb, i = args
        s = jnp.einsum('bhqd,bhkd->bhqk', qb, kh).astype(jnp.float32) * scale
        qpos = i * Q_BLOCK + jnp.arange(Q_BLOCK)
        s = jnp.where(kpos[None, :] <= qpos[:, None], s, -jnp.inf)
        p = jax.nn.softmax(s, axis=-1).reshape(b, DA_HEADS, 2, Q_BLOCK, L)
        w = (p[:, :, 0] - lam * p[:, :, 1]).astype(vh.dtype)
        return jnp.einsum('bhqk,bhkd->bhqd', w, vh)

    o = lax.map(block, (qblocks, jnp.arange(nblk)))
    o = o.transpose(1, 0, 3, 2, 4).reshape(b, L, DA_HEADS, DA_V_DIM)
    o = rms_norm(o, subln_w) * (1.0 - lambda_init)
    return o.reshape(b, L, DA_V_WIDTH)


def ssd_diffattn_mixer(h, w_in, conv_w, conv_b, dt_bias, a_log, d_skip, ssd_norm,
                       q_norm, k_norm, lq1, lk1, lq2, lk2, subln, w_out,
                       lambda_init, cos, sin):
    b, L, _ = h.shape
    proj = h @ w_in
    z, xbc, dt, q, k, v = jnp.split(proj, EVEN_SPLITS, axis=-1)
    xbc = jax.nn.silu(causal_dwconv(xbc, conv_w, conv_b))
    xs, Bm, Cm = jnp.split(xbc, [SSD_D_INNER, SSD_D_INNER + SSD_GROUPS * SSD_D_STATE], axis=-1)
    xs = xs.reshape(b, L, SSD_HEADS, SSD_HEAD_DIM).astype(jnp.float32)
    Bm = Bm.reshape(b, L, SSD_GROUPS, SSD_D_STATE).astype(jnp.float32)
    Cm = Cm.reshape(b, L, SSD_GROUPS, SSD_D_STATE).astype(jnp.float32)
    dtf = jax.nn.softplus(dt.astype(jnp.float32) + dt_bias.astype(jnp.float32))
    A = -jnp.exp(a_log.astype(jnp.float32))
    y = ssd_chunked(xs, dtf, A, Bm, Cm) + d_skip.astype(jnp.float32)[:, None] * xs
    y = y.reshape(b, L, SSD_D_INNER) * jax.nn.silu(z.astype(jnp.float32))
    yg = y.reshape(b, L, SSD_GROUPS, SSD_D_INNER // SSD_GROUPS)
    yg = yg * lax.rsqrt(jnp.mean(yg * yg, axis=-1, keepdims=True) + RMS_EPS)
    y_ssd = (yg.reshape(b, L, SSD_D_INNER) * ssd_norm.astype(jnp.float32)).astype(h.dtype)
    q = apply_rope(rms_norm(q.reshape(b, L, 2 * DA_HEADS, DA_HEAD_DIM), q_norm), cos, sin)
    k = apply_rope(rms_norm(k.reshape(b, L, 2 * DA_HEADS, DA_HEAD_DIM), k_norm), cos, sin)
    v = v.reshape(b, L, DA_HEADS, DA_V_DIM)
    lam = (jnp.exp(jnp.sum(lq1.astype(jnp.float32) * lk1.astype(jnp.float32)))
           - jnp.exp(jnp.sum(lq2.astype(jnp.float32) * lk2.astype(jnp.float32)))
           + lambda_init)
    y_att = diff_attention(q, k, v, lam, lambda_init, subln)
    return jnp.concatenate([y_ssd, y_att], axis=-1) @ w_out


def short_conv_mixer(h, w_in, conv_w, w_out):
    bg, cg, u = jnp.split(h @ w_in, 3, axis=-1)
    return (bg * causal_dwconv(cg * u, conv_w)) @ w_out


def conv_ffn(h, w_up, conv_w, conv_b, w_down):
    up = causal_dwconv(h @ w_up, conv_w, conv_b)
    gate, val = jnp.split(up, 2, axis=-1)
    return (jax.nn.silu(gate) * val) @ w_down


def setup_inputs(seed: int = 0) -> dict:
    key = jax.random.key(seed)
    ks = iter(jax.random.split(key, 40))

    def nrm(shape, scale):
        return jax.random.normal(next(ks), shape, jnp.float32) * scale

    def gain(shape):
        return 1.0 + nrm(shape, 0.02)

    x = nrm((BATCH, SEQ, D_MODEL), 1.0)
    mix_norm = gain((DEPTH, D_MODEL))
    ffn_norm = gain((DEPTH, D_MODEL))
    hy_w_in = nrm((N_EVEN, D_MODEL, EVEN_IN_WIDTH), D_MODEL ** -0.5)
    hy_conv_w = nrm((N_EVEN, SSD_CONV, SSD_CONV_DIM), SSD_CONV ** -0.5)
    hy_conv_b = nrm((N_EVEN, SSD_CONV_DIM), 0.01)
    u = jax.random.uniform(next(ks), (N_EVEN, SSD_HEADS), jnp.float32)
    dt0 = jnp.exp(u * (math.log(0.1) - math.log(0.001)) + math.log(0.001))
    hy_dt_bias = dt0 + jnp.log(-jnp.expm1(-dt0))
    hy_a_log = jnp.log(jax.random.uniform(next(ks), (N_EVEN, SSD_HEADS), jnp.float32, 1.0, 16.0))
    hy_d_skip = 1.0 + nrm((N_EVEN, SSD_HEADS), 0.1)
    hy_ssd_norm = gain((N_EVEN, SSD_D_INNER))
    hy_q_norm = gain((N_EVEN, DA_HEAD_DIM))
    hy_k_norm = gain((N_EVEN, DA_HEAD_DIM))
    hy_lambda_q1 = nrm((N_EVEN, DA_HEAD_DIM), 0.1)
    hy_lambda_k1 = nrm((N_EVEN, DA_HEAD_DIM), 0.1)
    hy_lambda_q2 = nrm((N_EVEN, DA_HEAD_DIM), 0.1)
    hy_lambda_k2 = nrm((N_EVEN, DA_HEAD_DIM), 0.1)
    hy_subln = gain((N_EVEN, DA_V_DIM))
    hy_w_out = nrm((N_EVEN, EVEN_OUT_WIDTH, D_MODEL), EVEN_OUT_WIDTH ** -0.5)
    sc_w_in = nrm((N_ODD, D_MODEL, 3 * SC_WIDTH), D_MODEL ** -0.5)
    sc_conv_w = nrm((N_ODD, SC_CONV, SC_WIDTH), SC_CONV ** -0.5)
    sc_w_out = nrm((N_ODD, SC_WIDTH, D_MODEL), SC_WIDTH ** -0.5)
    ffn_w_up = nrm((DEPTH, D_MODEL, 2 * D_FF), D_MODEL ** -0.5)
    ffn_conv_w = nrm((DEPTH, FFN_CONV, 2 * D_FF), FFN_CONV ** -0.5)
    ffn_conv_b = nrm((DEPTH, 2 * D_FF), 0.01)
    ffn_w_down = nrm((DEPTH, D_FF, D_MODEL), D_FF ** -0.5)
    return {"x": x, "mix_norm": mix_norm, "ffn_norm": ffn_norm,
            "hy_w_in": hy_w_in, "hy_conv_w": hy_conv_w, "hy_conv_b": hy_conv_b,
            "hy_dt_bias": hy_dt_bias, "hy_a_log": hy_a_log, "hy_d_skip": hy_d_skip,
            "hy_ssd_norm": hy_ssd_norm, "hy_q_norm": hy_q_norm, "hy_k_norm": hy_k_norm,
            "hy_lambda_q1": hy_lambda_q1, "hy_lambda_k1": hy_lambda_k1,
            "hy_lambda_q2": hy_lambda_q2, "hy_lambda_k2": hy_lambda_k2,
            "hy_subln": hy_subln, "hy_w_out": hy_w_out,
            "sc_w_in": sc_w_in, "sc_conv_w": sc_conv_w, "sc_w_out": sc_w_out,
            "ffn_w_up": ffn_w_up, "ffn_conv_w": ffn_conv_w, "ffn_conv_b": ffn_conv_b,
            "ffn_w_down": ffn_w_down}


def reference(x, mix_norm, ffn_norm, hy_w_in, hy_conv_w, hy_conv_b, hy_dt_bias, hy_a_log,
              hy_d_skip, hy_ssd_norm, hy_q_norm, hy_k_norm, hy_lambda_q1, hy_lambda_k1,
              hy_lambda_q2, hy_lambda_k2, hy_subln, hy_w_out, sc_w_in, sc_conv_w, sc_w_out,
              ffn_w_up, ffn_conv_w, ffn_conv_b, ffn_w_down):
    cos, sin = rope_tables(x.shape[1], DA_HEAD_DIM, x.dtype)
    h = x
    for layer in range(DEPTH):
        hn = rms_norm(h, mix_norm[layer])
        if layer % 2 == 0:
            e = layer // 2
            lambda_init = 0.8 - 0.6 * math.exp(-0.3 * layer)
            h = h + ssd_diffattn_mixer(hn, hy_w_in[e], hy_conv_w[e], hy_conv_b[e],
                                       hy_dt_bias[e], hy_a_log[e], hy_d_skip[e],
                                       hy_ssd_norm[e], hy_q_norm[e], hy_k_norm[e],
                                       hy_lambda_q1[e], hy_lambda_k1[e],
                                       hy_lambda_q2[e], hy_lambda_k2[e],
                                       hy_subln[e], hy_w_out[e], lambda_init, cos, sin)
        else:
            o = layer // 2
            h = h + short_conv_mixer(hn, sc_w_in[o], sc_conv_w[o], sc_w_out[o])
        h = h + conv_ffn(rms_norm(h, ffn_norm[layer]), ffn_w_up[layer], ffn_conv_w[layer],
                         ffn_conv_b[layer], ffn_w_down[layer])
    return h
```

```python
import functools
import math

import jax
import jax.numpy as jnp
from jax import lax
from jax.experimental import pallas as pl
from jax.experimental.pallas import tpu as pltpu

F32 = jnp.float32
BF16 = jnp.bfloat16

RMS_EPS = 1e-6
ROPE_THETA = 10000.0
D_MODEL = 1024
SSD_HEAD_DIM = 64
SSD_HEADS = 16
SSD_GROUPS = 2
SSD_D_STATE = 128
SSD_D_INNER = 1024
SSD_CONV = 4
SSD_CHUNK = 128
SSD_CONV_DIM = SSD_D_INNER + 2 * SSD_GROUPS * SSD_D_STATE
DA_HEADS = 8
DA_HEAD_DIM = 64
DA_V_DIM = 128
DA_WIDTH = 1024
SC_WIDTH = 1024
SC_CONV = 3
D_FF = 2816
FFN_CONV = 3
LANES = 128
SUBLANES = 8
VMEM_LIMIT = 56 * 1024 * 1024


def _dot(a, b):
    return jnp.dot(a, b, preferred_element_type=F32)


def _split3(x):
    hi = x.astype(BF16)
    r1 = x - hi.astype(F32)
    mid = r1.astype(BF16)
    lo = (r1 - mid.astype(F32)).astype(BF16)
    return hi, mid, lo


def _dot_sel_rhs(x, sel):
    hi, mid, lo = _split3(x)
    return _dot(hi, sel) + _dot(mid, sel) + _dot(lo, sel)


def _dot_sel_lhs(sel, x):
    hi, mid, lo = _split3(x)
    return _dot(sel, hi) + _dot(sel, mid) + _dot(sel, lo)


def _rms(x, w):
    return x * lax.rsqrt(jnp.mean(x * x, axis=-1, keepdims=True) + RMS_EPS) * w


def _silu(x):
    return x * (1.0 / (1.0 + jnp.exp(-x)))


def _softplus(x):
    return jnp.maximum(x, 0.0) + jnp.log1p(jnp.exp(-jnp.abs(x)))


def _const_spec(shape):
    nd = len(shape)
    return pl.BlockSpec(shape, lambda *_: (0,) * nd, pipeline_mode=pl.Buffered(1))


def _causal_conv(buf_ref, tm, taps, first_row):
    acc = None
    for k, w in enumerate(taps):
        term = buf_ref[pl.ds(first_row + k, tm), :] * w
        acc = term if acc is None else acc + term
    return acc


def _even_in_kernel(h_ref, nw_ref, wz_ref, wx_ref, wdt_ref, wq_ref, wk_ref, wv_ref,
                    cw_ref, cb_ref, dtb_ref, qn_ref, kn_ref, rope_ref, g_ref, gt_ref,
                    z_ref, xc_ref, dt_ref, q_ref, k_ref, v_ref, xs_ref):
    i = pl.program_id(1)
    tm = h_ref.shape[1]
    hn = _rms(h_ref[0], nw_ref[...]).astype(BF16)

    z_ref[0] = _dot(hn, wz_ref[...]).astype(BF16)
    v_ref[0] = _dot(hn, wv_ref[...]).astype(BF16)
    dt_ref[0] = _softplus(_dot(hn, wdt_ref[...]) + dtb_ref[...])

    @pl.when(i == 0)
    def _():
        xs_ref[0:SUBLANES, :] = jnp.zeros((SUBLANES, SSD_CONV_DIM), F32)

    xs_ref[SUBLANES:SUBLANES + tm, :] = _dot(hn, wx_ref[...])
    taps = [cw_ref[k:k + 1, :] for k in range(SSD_CONV)]
    conv = _causal_conv(xs_ref, tm, taps, SUBLANES - (SSD_CONV - 1)) + cb_ref[...]
    xc_ref[0] = _silu(conv).astype(BF16)
    xs_ref[0:SUBLANES, :] = xs_ref[tm:tm + SUBLANES, :]

    cos = jnp.tile(rope_ref[:, 0:LANES], (1, DA_WIDTH // LANES))
    sin_a = jnp.tile(rope_ref[:, LANES:2 * LANES], (1, DA_WIDTH // LANES))
    sin_b = jnp.tile(rope_ref[:, 2 * LANES:3 * LANES], (1, DA_WIDTH // LANES))
    half = DA_HEAD_DIM // 2
    for w_ref, n_ref, o_ref, scale in ((wq_ref, qn_ref, q_ref, DA_HEAD_DIM ** -0.5),
                                       (wk_ref, kn_ref, k_ref, 1.0)):
        x = _dot(hn, w_ref[...])
        ss = _dot((x * x).astype(BF16), g_ref[...])
        r = lax.rsqrt(ss * (1.0 / DA_HEAD_DIM) + RMS_EPS) * scale
        xn = x * _dot_sel_rhs(r, gt_ref[...]) * n_ref[...]
        rot = (pltpu.roll(xn, DA_WIDTH - half, 1) * sin_a + pltpu.roll(xn, half, 1) * sin_b)
        o_ref[0] = (xn * cos + rot).astype(BF16)


def _even_in(h, nw, wz, wx, wdt, wq, wk, wv, cw, cb, dtb, qn, kn, rope, g, gt, tm):
    B, L, D = h.shape
    grid = (B, L // tm)
    row = lambda w: pl.BlockSpec((1, tm, w), lambda b, i: (b, i, 0))
    out_shape = (jax.ShapeDtypeStruct((B, L, SSD_D_INNER), BF16),
                 jax.ShapeDtypeStruct((B, L, SSD_CONV_DIM), BF16),
                 jax.ShapeDtypeStruct((B, L, LANES), F32),
                 jax.ShapeDtypeStruct((B, L, DA_WIDTH), BF16),
                 jax.ShapeDtypeStruct((B, L, DA_WIDTH), BF16),
                 jax.ShapeDtypeStruct((B, L, DA_WIDTH), BF16))
    in_specs = [row(D), _const_spec(nw.shape), _const_spec(wz.shape), _const_spec(wx.shape),
                _const_spec(wdt.shape), _const_spec(wq.shape), _const_spec(wk.shape),
                _const_spec(wv.shape), _const_spec(cw.shape), _const_spec(cb.shape),
                _const_spec(dtb.shape), _const_spec(qn.shape), _const_spec(kn.shape),
                pl.BlockSpec((tm, 3 * LANES), lambda b, i: (i, 0)),
                _const_spec(g.shape), _const_spec(gt.shape)]
    out_specs = (row(SSD_D_INNER), row(SSD_CONV_DIM), row(LANES), row(DA_WIDTH), row(DA_WIDTH),
                 row(DA_WIDTH))
    return pl.pallas_call(
        _even_in_kernel, grid=grid, in_specs=in_specs, out_specs=out_specs, out_shape=out_shape,
        scratch_shapes=[pltpu.VMEM((tm + SUBLANES, SSD_CONV_DIM), F32)],
        compiler_params=pltpu.CompilerParams(dimension_semantics=("arbitrary", "arbitrary"),
                                             vmem_limit_bytes=VMEM_LIMIT),
        name="even_in",
    )(h, nw, wz, wx, wdt, wq, wk, wv, cw, cb, dtb, qn, kn, rope, g, gt)


def _ssd_kernel(xc_ref, dt_ref, z_ref, alog_ref, dskip_ref, nw_ref, y_ref, state_ref):
    c = pl.program_id(1)
    T = SSD_CHUNK
    GN = SSD_GROUPS * SSD_D_STATE
    GW = SSD_D_INNER // SSD_GROUPS

    @pl.when(c == 0)
    def _():
        state_ref[...] = jnp.zeros(state_ref.shape, F32)

    x = xc_ref[0, :, 0:SSD_D_INNER].astype(F32)
    bm = xc_ref[0, :, SSD_D_INNER:SSD_D_INNER + GN]
    cm = xc_ref[0, :, SSD_D_INNER + GN:SSD_D_INNER + 2 * GN]
    dt = dt_ref[0]
    a = -jnp.exp(alog_ref[...])
    adt = dt * a

    r_i = lax.broadcasted_iota(jnp.int32, (T, T), 0)
    c_i = lax.broadcasted_iota(jnp.int32, (T, T), 1)
    tril = r_i >= c_i
    tril_b = jnp.where(tril, 1.0, 0.0).astype(BF16)
    a_cs = _dot_sel_lhs(tril_b, adt)
    a_cs_t = a_cs.T
    a_last = a_cs[T - 1:T, :]

    e_r = lax.broadcasted_iota(jnp.int32, (LANES, SSD_D_INNER), 0)
    e_c = lax.broadcasted_iota(jnp.int32, (LANES, SSD_D_INNER), 1)
    expand = jnp.where(e_c // SSD_HEAD_DIM == e_r, 1.0, 0.0).astype(BF16)
    stacked = jnp.concatenate([dt, jnp.exp(a_cs), jnp.exp(a_last - a_cs)], axis=0)
    ex = _dot_sel_rhs(stacked, expand)
    dt_e, expa_e, decay_e = ex[0:T], ex[T:2 * T], ex[2 * T:3 * T]
    xdt = x * dt_e

    b_r = lax.broadcasted_iota(jnp.int32, (LANES, SSD_HEADS * T), 0)
    b_c = lax.broadcasted_iota(jnp.int32, (LANES, SSD_HEADS * T), 1)
    colsel = jnp.where(b_c // T == b_r, 1.0, 0.0).astype(BF16)
    a_col = _dot_sel_rhs(a_cs, colsel)

    lane = lax.broadcasted_iota(jnp.int32, (T, LANES), 1)
    xdt_b = xdt.astype(BF16)
    y_parts = []
    for g in range(SSD_GROUPS):
        cg = cm[:, g * SSD_D_STATE:(g + 1) * SSD_D_STATE]
        bg = bm[:, g * SSD_D_STATE:(g + 1) * SSD_D_STATE]
        cb = lax.dot_general(cg, bg, (((1,), (1,)), ((), ())), preferred_element_type=F32)
        heads_per_group = SSD_HEADS // SSD_GROUPS
        for hp in range(heads_per_group // 2):
            h0 = g * heads_per_group + 2 * hp
            ms = []
            for h in (h0, h0 + 1):
                seg = a_col[:, h * T:(h + 1) * T] - a_cs_t[h:h + 1, :]
                ms.append((cb * jnp.exp(jnp.where(tril, seg, -jnp.inf))).astype(BF16))
            lhs = jnp.concatenate(ms, axis=1)
            xp = xdt_b[:, h0 * SSD_HEAD_DIM:(h0 + 2) * SSD_HEAD_DIM]
            rhs = jnp.concatenate([jnp.where(lane < SSD_HEAD_DIM, xp, 0),
                                   jnp.where(lane >= SSD_HEAD_DIM, xp, 0)], axis=0)
            y_parts.append(_dot(lhs, rhs))
    y_diag = jnp.concatenate(y_parts, axis=1)

    y_off_parts = []
    for g in range(SSD_GROUPS):
        cg = cm[:, g * SSD_D_STATE:(g + 1) * SSD_D_STATE]
        y_off_parts.append(_dot(cg, state_ref[g].astype(BF16)))
    y_off = jnp.concatenate(y_off_parts, axis=1) * expa_e

    xdec = (xdt * decay_e).astype(BF16)
    chunk_decay = expa_e[T - 1:T, :]
    for g in range(SSD_GROUPS):
        bg_t = bm[:, g * SSD_D_STATE:(g + 1) * SSD_D_STATE].astype(F32).T.astype(BF16)
        upd = _dot(bg_t, xdec[:, g * GW:(g + 1) * GW])
        state_ref[g] = state_ref[g] * chunk_decay[:, g * GW:(g + 1) * GW] + upd

    y = y_diag + y_off + dskip_ref[...] * x
    y = y * _silu(z_ref[0].astype(F32))
    outs = []
    for g in range(SSD_GROUPS):
        yg = y[:, g * GW:(g + 1) * GW]
        outs.append(yg * lax.rsqrt(jnp.mean(yg * yg, axis=-1, keepdims=True) + RMS_EPS))
    y_ref[0] = (jnp.concatenate(outs, axis=1) * nw_ref[...]).astype(BF16)


def _ssd(xc, dtf, z, alog, dskip, nw):
    B, L, _ = xc.shape
    T = SSD_CHUNK
    grid = (B, L // T)
    row = lambda w: pl.BlockSpec((1, T, w), lambda b, c: (b, c, 0))
    return pl.pallas_call(
        _ssd_kernel, grid=grid,
        in_specs=[row(SSD_CONV_DIM), row(LANES), row(SSD_D_INNER),
                  _const_spec(alog.shape), _const_spec(dskip.shape), _const_spec(nw.shape)],
        out_specs=row(SSD_D_INNER),
        out_shape=jax.ShapeDtypeStruct((B, L, SSD_D_INNER), BF16),
        scratch_shapes=[pltpu.VMEM((SSD_GROUPS, SSD_D_STATE, SSD_D_INNER // SSD_GROUPS), F32)],
        compiler_params=pltpu.CompilerParams(dimension_semantics=("arbitrary", "arbitrary"),
                                             vmem_limit_bytes=VMEM_LIMIT),
        name="ssd",
    )(xc, dtf, z, alog, dskip, nw)


def _attn_kernel(q_ref, k_ref, v_ref, lq1_ref, lk1_ref, lq2_ref, lk2_ref, sub_ref, o_ref,
                 *, tq, lambda_init):
    i = pl.program_id(2)
    q = q_ref[0]
    lane = lax.broadcasted_iota(jnp.int32, q.shape, 1)
    zero = jnp.zeros_like(q)
    q2 = jnp.concatenate([jnp.where(lane < DA_HEAD_DIM, q, zero),
                          jnp.where(lane >= DA_HEAD_DIM, q, zero)], axis=0)

    def step(j, carry, masked):
        m, l, acc = carry
        start = pl.multiple_of(j * tq, tq)
        kb = k_ref[0, pl.ds(start, tq), :]
        vb = v_ref[0, pl.ds(start, tq), :]
        s = lax.dot_general(q2, kb, (((1,), (1,)), ((), ())), preferred_element_type=F32)
        if masked:
            row = lax.broadcasted_iota(jnp.int32, s.shape, 0)
            col = lax.broadcasted_iota(jnp.int32, s.shape, 1)
            qpos = jnp.where(row >= tq, row - tq, row)
            s = jnp.where(col <= qpos, s, -jnp.inf)
        m_new = jnp.maximum(m, jnp.max(s, axis=-1, keepdims=True))
        p = jnp.exp(s - m_new)
        alpha = jnp.exp(m - m_new)
        l_new = alpha * l + jnp.sum(p, axis=-1, keepdims=True)
        acc_new = alpha * acc + _dot(p.astype(BF16), vb)
        return m_new, l_new, acc_new

    init = (jnp.full((2 * tq, 1), -jnp.inf, F32), jnp.zeros((2 * tq, 1), F32),
            jnp.zeros((2 * tq, DA_V_DIM), F32))
    carry = lax.fori_loop(0, i, lambda j, c: step(j, c, False), init)
    _, l, acc = step(i, carry, True)

    lam = (jnp.exp(jnp.sum(lq1_ref[...] * lk1_ref[...], axis=-1, keepdims=True))
           - jnp.exp(jnp.sum(lq2_ref[...] * lk2_ref[...], axis=-1, keepdims=True)) + lambda_init)
    o = acc[0:tq] / l[0:tq] - lam * (acc[tq:2 * tq] / l[tq:2 * tq])
    o = _rms(o, sub_ref[...]) * (1.0 - lambda_init)
    o_ref[0] = o.astype(BF16)


def _diff_attn(q, k, v, lq1, lk1, lq2, lk2, sub, lambda_init, tq):
    B, L, _ = q.shape
    grid = (B, DA_HEADS, L // tq)
    full = pl.BlockSpec((1, L, DA_V_DIM), lambda b, h, i: (b, 0, h))
    blk = pl.BlockSpec((1, tq, DA_V_DIM), lambda b, h, i: (b, i, h))
    return pl.pallas_call(
        functools.partial(_attn_kernel, tq=tq, lambda_init=lambda_init),
        grid=grid,
        in_specs=[blk, full, full, _const_spec(lq1.shape), _const_spec(lk1.shape),
                  _const_spec(lq2.shape), _const_spec(lk2.shape), _const_spec(sub.shape)],
        out_specs=blk,
        out_shape=jax.ShapeDtypeStruct((B, L, DA_WIDTH), BF16),
        compiler_params=pltpu.CompilerParams(
            dimension_semantics=("arbitrary", "arbitrary", "arbitrary"),
            vmem_limit_bytes=VMEM_LIMIT),
        name="diff_attn",
    )(q, k, v, lq1, lk1, lq2, lk2, sub)


def _ffn_tile(o_ref, i, nw_ref, wup_ref, cw_ref, cb_ref, wdn_ref, buf_ref, tm):
    hn = _rms(o_ref[0], nw_ref[...]).astype(BF16)
    half = D_FF // 2
    lead = SUBLANES - (FFN_CONV - 1)

    @pl.when(i == 0)
    def _():
        buf_ref[:, 0:SUBLANES, :] = jnp.zeros((4, SUBLANES, half), F32)

    for c in range(2):
        acts = []
        for part in range(2):
            col = part * D_FF + c * half
            slot = 2 * c + part
            buf_ref[slot, SUBLANES:SUBLANES + tm, :] = _dot(hn, wup_ref[:, col:col + half])
            taps = [cw_ref[k:k + 1, col:col + half] for k in range(FFN_CONV)]
            conv = _causal_conv(buf_ref.at[slot], tm, taps, lead) + cb_ref[:, col:col + half]
            buf_ref[slot, 0:SUBLANES, :] = buf_ref[slot, tm:tm + SUBLANES, :]
            acts.append(conv)
        act = (_silu(acts[0]) * acts[1]).astype(BF16)
        o_ref[0] += _dot(act, wdn_ref[c * half:(c + 1) * half, :])


def _out_ffn_kernel(h_ref, ys_ref, ya_ref, wo_ref, nw_ref, wup_ref, cw_ref, cb_ref, wdn_ref,
                    o_ref, buf_ref):
    i = pl.program_id(1)
    tm = h_ref.shape[1]
    o_ref[0] = (h_ref[0] + _dot(ys_ref[0], wo_ref[0:SSD_D_INNER, :])
                + _dot(ya_ref[0], wo_ref[SSD_D_INNER:SSD_D_INNER + DA_WIDTH, :]))
    _ffn_tile(o_ref, i, nw_ref, wup_ref, cw_ref, cb_ref, wdn_ref, buf_ref, tm)


def _out_ffn(h, ys, ya, wo, nw, wup, cw, cb, wdn, tm):
    B, L, D = h.shape
    grid = (B, L // tm)
    row = lambda w: pl.BlockSpec((1, tm, w), lambda b, i: (b, i, 0))
    return pl.pallas_call(
        _out_ffn_kernel, grid=grid,
        in_specs=[row(D), row(SSD_D_INNER), row(DA_WIDTH), _const_spec(wo.shape),
                  _const_spec(nw.shape), _const_spec(wup.shape), _const_spec(cw.shape),
                  _const_spec(cb.shape), _const_spec(wdn.shape)],
        out_specs=row(D),
        out_shape=jax.ShapeDtypeStruct((B, L, D), F32),
        scratch_shapes=[pltpu.VMEM((4, tm + SUBLANES, D_FF // 2), F32)],
        compiler_params=pltpu.CompilerParams(dimension_semantics=("arbitrary", "arbitrary"),
                                             vmem_limit_bytes=VMEM_LIMIT),
        name="out_ffn",
    )(h, ys, ya, wo, nw, wup, cw, cb, wdn)


def _sc_ffn_kernel(h_ref, mnw_ref, win_ref, scw_ref, wout_ref, nw_ref, wup_ref, cw_ref, cb_ref,
                   wdn_ref, o_ref, mbuf_ref, buf_ref):
    i = pl.program_id(1)
    tm = h_ref.shape[1]
    h = h_ref[0]
    hn = _rms(h, mnw_ref[...]).astype(BF16)

    @pl.when(i == 0)
    def _():
        mbuf_ref[0:SUBLANES, :] = jnp.zeros((SUBLANES, SC_WIDTH), F32)

    cg = _dot(hn, win_ref[:, SC_WIDTH:2 * SC_WIDTH])
    u = _dot(hn, win_ref[:, 2 * SC_WIDTH:3 * SC_WIDTH])
    mbuf_ref[SUBLANES:SUBLANES + tm, :] = cg * u
    taps = [scw_ref[k:k + 1, :] for k in range(SC_CONV)]
    conv = _causal_conv(mbuf_ref, tm, taps, SUBLANES - (SC_CONV - 1))
    mbuf_ref[0:SUBLANES, :] = mbuf_ref[tm:tm + SUBLANES, :]
    bg = _dot(hn, win_ref[:, 0:SC_WIDTH])
    o_ref[0] = h + _dot((bg * conv).astype(BF16), wout_ref[...])
    _ffn_tile(o_ref, i, nw_ref, wup_ref, cw_ref, cb_ref, wdn_ref, buf_ref, tm)


def _sc_ffn(h, mnw, win, scw, wout, nw, wup, cw, cb, wdn, tm):
    B, L, D = h.shape
    grid = (B, L // tm)
    row = pl.BlockSpec((1, tm, D), lambda b, i: (b, i, 0))
    return pl.pallas_call(
        _sc_ffn_kernel, grid=grid,
        in_specs=[row, _const_spec(mnw.shape), _const_spec(win.shape), _const_spec(scw.shape),
                  _const_spec(wout.shape), _const_spec(nw.shape), _const_spec(wup.shape),
                  _const_spec(cw.shape), _const_spec(cb.shape), _const_spec(wdn.shape)],
        out_specs=row,
        out_shape=jax.ShapeDtypeStruct((B, L, D), F32),
        scratch_shapes=[pltpu.VMEM((tm + SUBLANES, SC_WIDTH), F32),
                        pltpu.VMEM((4, tm + SUBLANES, D_FF // 2), F32)],
        compiler_params=pltpu.CompilerParams(dimension_semantics=("arbitrary", "arbitrary"),
                                             vmem_limit_bytes=VMEM_LIMIT),
        name="sc_ffn",
    )(h, mnw, win, scw, wout, nw, wup, cw, cb, wdn)


def _rope_table(seq):
    inv = 1.0 / (ROPE_THETA ** (jnp.arange(0, DA_HEAD_DIM, 2, dtype=F32) / DA_HEAD_DIM))
    ang = jnp.arange(seq, dtype=F32)[:, None] * inv[None, :]
    ang = jnp.concatenate([ang, ang], axis=-1)
    cos, sin = jnp.cos(ang), jnp.sin(ang)
    first = jnp.arange(DA_HEAD_DIM) < DA_HEAD_DIM // 2
    sin_a = jnp.where(first, -sin, 0.0)
    sin_b = jnp.where(first, 0.0, sin)
    rep = LANES // DA_HEAD_DIM
    return jnp.concatenate([jnp.tile(cos, (1, rep)), jnp.tile(sin_a, (1, rep)),
                            jnp.tile(sin_b, (1, rep))], axis=-1)


def _row(v):
    return v.reshape(1, -1).astype(F32)


def _pad_lanes(v):
    return jnp.pad(v.reshape(1, -1).astype(F32), ((0, 0), (0, LANES - v.shape[-1])))


def kernel(x, mix_norm, ffn_norm, hy_w_in, hy_conv_w, hy_conv_b, hy_dt_bias, hy_a_log, hy_d_skip,
           hy_ssd_norm, hy_q_norm, hy_k_norm, hy_lambda_q1, hy_lambda_k1, hy_lambda_q2,
           hy_lambda_k2, hy_subln, hy_w_out, sc_w_in, sc_conv_w, sc_w_out, ffn_w_up, ffn_conv_w,
           ffn_conv_b, ffn_w_down):
    B, L, D = x.shape
    tm = min(512, L)
    tq = min(512, L)

    w = hy_w_in[0]
    o_x = SSD_D_INNER
    o_dt = o_x + SSD_CONV_DIM
    o_q = o_dt + SSD_HEADS
    o_k = o_q + DA_WIDTH
    o_v = o_k + DA_WIDTH
    wz = w[:, 0:o_x].astype(BF16)
    wx = w[:, o_x:o_dt].astype(BF16)
    wdt = jnp.pad(w[:, o_dt:o_q], ((0, 0), (0, LANES - SSD_HEADS))).astype(BF16)
    wq = w[:, o_q:o_k].astype(BF16)
    wk = w[:, o_k:o_v].astype(BF16)
    wv = w[:, o_v:o_v + DA_WIDTH].astype(BF16)
    head_of = jnp.arange(DA_WIDTH) // DA_HEAD_DIM
    g = (head_of[:, None] == jnp.arange(LANES)[None, :]).astype(BF16)
    gt = g.T
    rope = _rope_table(L)
    z, xc, dtf, q, k, v = _even_in(
        x, _row(mix_norm[0]), wz, wx, wdt, wq, wk, wv, hy_conv_w[0].astype(F32),
        _row(hy_conv_b[0]), _pad_lanes(hy_dt_bias[0]),
        _row(jnp.tile(hy_q_norm[0], DA_WIDTH // DA_HEAD_DIM)),
        _row(jnp.tile(hy_k_norm[0], DA_WIDTH // DA_HEAD_DIM)), rope, g, gt, tm)
    y_ssd = _ssd(xc, dtf, z, _pad_lanes(hy_a_log[0]),
                 _row(jnp.repeat(hy_d_skip[0], SSD_HEAD_DIM)), _row(hy_ssd_norm[0]))
    lambda_init = 0.8 - 0.6 * math.exp(-0.3 * 0)
    y_att = _diff_attn(q, k, v, _row(hy_lambda_q1[0]), _row(hy_lambda_k1[0]),
                       _row(hy_lambda_q2[0]), _row(hy_lambda_k2[0]), _row(hy_subln[0]),
                       lambda_init, tq)
    h = _out_ffn(x, y_ssd, y_att, hy_w_out[0].astype(BF16), _row(ffn_norm[0]),
                 ffn_w_up[0].astype(BF16), ffn_conv_w[0].astype(F32), _row(ffn_conv_b[0]),
                 ffn_w_down[0].astype(BF16), tm)

    h = _sc_ffn(h, _row(mix_norm[1]), sc_w_in[0].astype(BF16), sc_conv_w[0].astype(F32),
                sc_w_out[0].astype(BF16), _row(ffn_norm[1]), ffn_w_up[1].astype(BF16),
                ffn_conv_w[1].astype(F32), _row(ffn_conv_b[1]), ffn_w_down[1].astype(BF16), tm)
    return h
```

```python
import functools
import math

import jax
import jax.numpy as jnp
from jax import lax
from jax.experimental import pallas as pl
from jax.experimental.pallas import tpu as pltpu

F32 = jnp.float32
BF16 = jnp.bfloat16

RMS_EPS = 1e-6
ROPE_THETA = 10000.0
D_MODEL = 1024
SSD_HEAD_DIM = 64
SSD_HEADS = 16
SSD_GROUPS = 2
SSD_D_STATE = 128
SSD_D_INNER = 1024
SSD_CONV = 4
SSD_CHUNK = 128
SSD_CONV_DIM = SSD_D_INNER + 2 * SSD_GROUPS * SSD_D_STATE
DA_HEADS = 8
DA_HEAD_DIM = 64
DA_V_DIM = 128
DA_WIDTH = 1024
SC_WIDTH = 1024
SC_CONV = 3
D_FF = 2816
FFN_CONV = 3
LANES = 128
SUBLANES = 8
VMEM_LIMIT = 56 * 1024 * 1024
LOG2E = 1.4426950408889634


def _dot(a, b):
    return jnp.dot(a, b, preferred_element_type=F32)


def _split3(x):
    hi = x.astype(BF16)
    r1 = x - hi.astype(F32)
    mid = r1.astype(BF16)
    lo = (r1 - mid.astype(F32)).astype(BF16)
    return hi, mid, lo


def _dot_sel_rhs(x, sel):
    hi, mid, lo = _split3(x)
    return _dot(hi, sel) + _dot(mid, sel) + _dot(lo, sel)


def _dot_sel_rhs16(x, sel2):
    hi = x.astype(BF16)
    mid = (x - hi.astype(F32)).astype(BF16)
    return _dot(jnp.concatenate([hi, mid], axis=1), sel2)


def _dot_sel_lhs(sel, x):
    hi, mid, lo = _split3(x)
    return _dot(sel, hi) + _dot(sel, mid) + _dot(sel, lo)


def _rms(x, w):
    return x * lax.rsqrt(jnp.mean(x * x, axis=-1, keepdims=True) + RMS_EPS) * w


def _silu(x):
    return x * (1.0 / (1.0 + jnp.exp(-x)))


def _softplus(x):
    return jnp.maximum(x, 0.0) + jnp.log1p(jnp.exp(-jnp.abs(x)))


def _const_spec(shape):
    nd = len(shape)
    return pl.BlockSpec(shape, lambda *_: (0,) * nd, pipeline_mode=pl.Buffered(1))


def _causal_conv(buf_ref, tm, taps, first_row):
    acc = None
    for k, w in enumerate(taps):
        term = buf_ref[pl.ds(first_row + k, tm), :] * w
        acc = term if acc is None else acc + term
    return acc


def _even_in_kernel(h_ref, nw_ref, wz_ref, wx_ref, wdt_ref, wq_ref, wk_ref, wv_ref,
                    cw_ref, cb_ref, dtb_ref, qn_ref, kn_ref, rope_ref, g_ref, gt_ref,
                    z_ref, xc_ref, dt_ref, q_ref, k_ref, v_ref, xs_ref):
    i = pl.program_id(1)
    tm = h_ref.shape[1]
    hn = _rms(h_ref[0], nw_ref[...]).astype(BF16)

    z_ref[0] = _dot(hn, wz_ref[...]).astype(BF16)
    v_ref[0] = _dot(hn, wv_ref[...]).astype(BF16)
    dt_ref[0] = _softplus(_dot(hn, wdt_ref[...]) + dtb_ref[...])

    @pl.when(i == 0)
    def _():
        xs_ref[0:SUBLANES, :] = jnp.zeros((SUBLANES, SSD_CONV_DIM), F32)

    xs_ref[SUBLANES:SUBLANES + tm, :] = _dot(hn, wx_ref[...])
    taps = [cw_ref[k:k + 1, :] for k in range(SSD_CONV)]
    conv = _causal_conv(xs_ref, tm, taps, SUBLANES - (SSD_CONV - 1)) + cb_ref[...]
    xc_ref[0] = _silu(conv).astype(BF16)
    xs_ref[0:SUBLANES, :] = xs_ref[tm:tm + SUBLANES, :]

    cos = jnp.tile(rope_ref[:, 0:LANES], (1, DA_WIDTH // LANES))
    sin_a = jnp.tile(rope_ref[:, LANES:2 * LANES], (1, DA_WIDTH // LANES))
    sin_b = jnp.tile(rope_ref[:, 2 * LANES:3 * LANES], (1, DA_WIDTH // LANES))
    half = DA_HEAD_DIM // 2
    for w_ref, n_ref, o_ref, scale in ((wq_ref, qn_ref, q_ref, LOG2E * DA_HEAD_DIM ** -0.5),
                                       (wk_ref, kn_ref, k_ref, 1.0)):
        x = _dot(hn, w_ref[...])
        ss = _dot((x * x).astype(BF16), g_ref[...])
        r = lax.rsqrt(ss * (1.0 / DA_HEAD_DIM) + RMS_EPS) * scale
        xn = x * _dot_sel_rhs(r, gt_ref[...]) * n_ref[...]
        rot = (pltpu.roll(xn, DA_WIDTH - half, 1) * sin_a + pltpu.roll(xn, half, 1) * sin_b)
        o_ref[0] = (xn * cos + rot).astype(BF16)


def _even_in(h, nw, wz, wx, wdt, wq, wk, wv, cw, cb, dtb, qn, kn, rope, g, gt, tm):
    B, L, D = h.shape
    grid = (B, L // tm)
    row = lambda w: pl.BlockSpec((1, tm, w), lambda b, i: (b, i, 0))
    out_shape = (jax.ShapeDtypeStruct((B, L, SSD_D_INNER), BF16),
                 jax.ShapeDtypeStruct((B, L, SSD_CONV_DIM), BF16),
                 jax.ShapeDtypeStruct((B, L, LANES), F32),
                 jax.ShapeDtypeStruct((B, L, DA_WIDTH), BF16),
                 jax.ShapeDtypeStruct((B, L, DA_WIDTH), BF16),
                 jax.ShapeDtypeStruct((B, L, DA_WIDTH), BF16))
    in_specs = [row(D), _const_spec(nw.shape), _const_spec(wz.shape), _const_spec(wx.shape),
                _const_spec(wdt.shape), _const_spec(wq.shape), _const_spec(wk.shape),
                _const_spec(wv.shape), _const_spec(cw.shape), _const_spec(cb.shape),
                _const_spec(dtb.shape), _const_spec(qn.shape), _const_spec(kn.shape),
                pl.BlockSpec((tm, 3 * LANES), lambda b, i: (i, 0)),
                _const_spec(g.shape), _const_spec(gt.shape)]
    out_specs = (row(SSD_D_INNER), row(SSD_CONV_DIM), row(LANES), row(DA_WIDTH), row(DA_WIDTH),
                 row(DA_WIDTH))
    return pl.pallas_call(
        _even_in_kernel, grid=grid, in_specs=in_specs, out_specs=out_specs, out_shape=out_shape,
        scratch_shapes=[pltpu.VMEM((tm + SUBLANES, SSD_CONV_DIM), F32)],
        compiler_params=pltpu.CompilerParams(dimension_semantics=("arbitrary", "arbitrary"),
                                             vmem_limit_bytes=VMEM_LIMIT),
        name="even_in",
    )(h, nw, wz, wx, wdt, wq, wk, wv, cw, cb, dtb, qn, kn, rope, g, gt)


def _ssd_kernel(xc_ref, dt_ref, z_ref, alog_ref, dskip_ref, nw_ref, tril_ref, expand_ref,
                y_ref, state_ref):
    T = SSD_CHUNK
    GN = SSD_GROUPS * SSD_D_STATE
    GW = SSD_D_INNER // SSD_GROUPS
    heads_per_group = SSD_HEADS // SSD_GROUPS

    @pl.when(pl.program_id(1) == 0)
    def _():
        state_ref[...] = jnp.zeros(state_ref.shape, F32)

    a = -jnp.exp(alog_ref[...])
    r_i = lax.broadcasted_iota(jnp.int32, (T, T), 0)
    c_i = lax.broadcasted_iota(jnp.int32, (T, T), 1)
    tril = r_i >= c_i
    lane = lax.broadcasted_iota(jnp.int32, (T, LANES), 1)

    for ci in range(xc_ref.shape[1] // T):
        rows = slice(ci * T, (ci + 1) * T)
        x = xc_ref[0, rows, 0:SSD_D_INNER].astype(F32)
        bm = xc_ref[0, rows, SSD_D_INNER:SSD_D_INNER + GN]
        cm = xc_ref[0, rows, SSD_D_INNER + GN:SSD_D_INNER + 2 * GN]
        dt = dt_ref[0, rows, :]
        a_cs = _dot_sel_lhs(tril_ref[...], dt * a)
        a_cs_t = a_cs.T
        a_last = a_cs[T - 1:T, :]

        stacked = jnp.concatenate(
            [dt, jnp.exp(a_cs), jnp.exp(a_last - a_cs),
             jnp.broadcast_to(jnp.exp(a_last), (SUBLANES, LANES))], axis=0)
        ex = _dot_sel_rhs16(stacked, expand_ref[...])
        dt_e, expa_e, decay_e = ex[0:T], ex[T:2 * T], ex[2 * T:3 * T]
        chunk_decay = ex[3 * T:3 * T + 1]
        xdt = x * dt_e

        xdt_b = xdt.astype(BF16)
        y_parts = []
        for g in range(SSD_GROUPS):
            cg = cm[:, g * SSD_D_STATE:(g + 1) * SSD_D_STATE]
            bg = bm[:, g * SSD_D_STATE:(g + 1) * SSD_D_STATE]
            cb = lax.dot_general(cg, bg, (((1,), (1,)), ((), ())), preferred_element_type=F32)
            for hp in range(heads_per_group // 2):
                h0 = g * heads_per_group + 2 * hp
                ms = []
                for h in (h0, h0 + 1):
                    seg = a_cs[:, h:h + 1] - a_cs_t[h:h + 1, :]
                    ms.append((cb * jnp.exp(jnp.where(tril, seg, -jnp.inf))).astype(BF16))
                lhs = jnp.concatenate(ms, axis=1)
                xp = xdt_b[:, h0 * SSD_HEAD_DIM:(h0 + 2) * SSD_HEAD_DIM]
                rhs = jnp.concatenate([jnp.where(lane < SSD_HEAD_DIM, xp, 0),
                                       jnp.where(lane >= SSD_HEAD_DIM, xp, 0)], axis=0)
                y_parts.append(_dot(lhs, rhs))
        y_diag = jnp.concatenate(y_parts, axis=1)

        xdec = (xdt * decay_e).astype(BF16)
        y_off_parts = []
        for g in range(SSD_GROUPS):
            cg = cm[:, g * SSD_D_STATE:(g + 1) * SSD_D_STATE]
            y_off_parts.append(_dot(cg, state_ref[g].astype(BF16)))
            bg_t = bm[:, g * SSD_D_STATE:(g + 1) * SSD_D_STATE].astype(F32).T.astype(BF16)
            upd = _dot(bg_t, xdec[:, g * GW:(g + 1) * GW])
            state_ref[g] = state_ref[g] * chunk_decay[:, g * GW:(g + 1) * GW] + upd
        y_off = jnp.concatenate(y_off_parts, axis=1) * expa_e

        y = y_diag + y_off + dskip_ref[...] * x
        y = y * _silu(z_ref[0, rows, :].astype(F32))
        outs = []
        for g in range(SSD_GROUPS):
            yg = y[:, g * GW:(g + 1) * GW]
            outs.append(yg * lax.rsqrt(jnp.mean(yg * yg, axis=-1, keepdims=True) + RMS_EPS))
        y_ref[0, rows, :] = (jnp.concatenate(outs, axis=1) * nw_ref[...]).astype(BF16)


def _ssd(xc, dtf, z, alog, dskip, nw, ts):
    B, L, _ = xc.shape
    T = SSD_CHUNK
    grid = (B, L // ts)
    row = lambda w: pl.BlockSpec((1, ts, w), lambda b, c: (b, c, 0))
    tril = (jnp.arange(T)[:, None] >= jnp.arange(T)[None, :]).astype(BF16)
    expand = (jnp.arange(SSD_D_INNER)[None, :] // SSD_HEAD_DIM
              == jnp.arange(2 * LANES)[:, None] % LANES).astype(BF16)
    return pl.pallas_call(
        _ssd_kernel, grid=grid,
        in_specs=[row(SSD_CONV_DIM), row(LANES), row(SSD_D_INNER),
                  _const_spec(alog.shape), _const_spec(dskip.shape), _const_spec(nw.shape),
                  _const_spec(tril.shape), _const_spec(expand.shape)],
        out_specs=row(SSD_D_INNER),
        out_shape=jax.ShapeDtypeStruct((B, L, SSD_D_INNER), BF16),
        scratch_shapes=[pltpu.VMEM((SSD_GROUPS, SSD_D_STATE, SSD_D_INNER // SSD_GROUPS), F32)],
        compiler_params=pltpu.CompilerParams(dimension_semantics=("arbitrary", "arbitrary"),
                                             vmem_limit_bytes=VMEM_LIMIT),
        name="ssd",
    )(xc, dtf, z, alog, dskip, nw, tril, expand)


def _attn_kernel(q_ref, k_ref, v_ref, lq1_ref, lk1_ref, lq2_ref, lk2_ref, sub_ref, o_ref, q2_ref,
                 *, tq, lambda_init):
    i = pl.program_id(2)
    q = q_ref[0]
    lane = lax.broadcasted_iota(jnp.int32, q.shape, 1)
    zero = jnp.zeros_like(q)
    q2_ref[0:tq, :] = jnp.where(lane < DA_HEAD_DIM, q, zero)
    q2_ref[tq:2 * tq, :] = jnp.where(lane >= DA_HEAD_DIM, q, zero)
    def scores(j, masked):
        start = pl.multiple_of(j * tq, tq)
        kb = k_ref[0, pl.ds(start, tq), :]
        s = lax.dot_general(kb, q2_ref[...], (((1,), (1,)), ((), ())),
                            preferred_element_type=F32)
        if masked:
            kpos = lax.broadcasted_iota(jnp.int32, s.shape, 0)
            qcol = lax.broadcasted_iota(jnp.int32, s.shape, 1)
            qpos = jnp.where(qcol >= tq, qcol - tq, qcol)
            s = jnp.where(kpos <= qpos, s, -jnp.inf)
        return s

    def update(j, s, carry):
        m, l, acc = carry
        start = pl.multiple_of(j * tq, tq)
        vb = v_ref[0, pl.ds(start, tq), :]
        m_new = jnp.maximum(m, jnp.max(s, axis=0, keepdims=True))
        p = jnp.exp2(s - m_new)
        alpha = jnp.exp2(m - m_new)
        l_new = alpha * l + jnp.sum(p, axis=0, keepdims=True)
        pv = lax.dot_general(vb, p.astype(BF16), (((0,), (0,)), ((), ())),
                             preferred_element_type=F32)
        return m_new, l_new, alpha * acc + pv

    def pair(j0, carry, masked_second):
        s_a = scores(j0, False)
        s_b = scores(j0 + 1, masked_second)
        return update(j0 + 1, s_b, update(j0, s_a, carry))

    def single_masked(j, carry):
        return update(j, scores(j, True), carry)

    init = (jnp.full((1, 2 * tq), -jnp.inf, F32), jnp.zeros((1, 2 * tq), F32),
            jnp.zeros((DA_V_DIM, 2 * tq), F32))
    n_blocks = i + 1
    carry = lax.fori_loop(0, (n_blocks - 1) // 2, lambda t, c: pair(2 * t, c, False), init)
    _, l, acc = lax.cond(n_blocks % 2 == 0,
                         lambda c: pair(i - 1, c, True),
                         lambda c: single_masked(i, c), carry)
    o_t = acc / l

    lam = (jnp.exp(jnp.sum(lq1_ref[...] * lk1_ref[...], axis=-1, keepdims=True))
           - jnp.exp(jnp.sum(lq2_ref[...] * lk2_ref[...], axis=-1, keepdims=True)) + lambda_init)
    o = (o_t[:, 0:tq] - lam * o_t[:, tq:2 * tq]).T
    o = _rms(o, sub_ref[...]) * (1.0 - lambda_init)
    o_ref[0] = o.astype(BF16)


def _diff_attn(q, k, v, lq1, lk1, lq2, lk2, sub, lambda_init, tq):
    B, L, _ = q.shape
    grid = (B, DA_HEADS, L // tq)
    full = pl.BlockSpec((1, L, DA_V_DIM), lambda b, h, i: (b, 0, h))
    blk = pl.BlockSpec((1, tq, DA_V_DIM), lambda b, h, i: (b, i, h))
    return pl.pallas_call(
        functools.partial(_attn_kernel, tq=tq, lambda_init=lambda_init),
        grid=grid,
        scratch_shapes=[pltpu.VMEM((2 * tq, DA_V_DIM), BF16)],
        in_specs=[blk, full, full, _const_spec(lq1.shape), _const_spec(lk1.shape),
                  _const_spec(lq2.shape), _const_spec(lk2.shape), _const_spec(sub.shape)],
        out_specs=blk,
        out_shape=jax.ShapeDtypeStruct((B, L, DA_WIDTH), BF16),
        compiler_params=pltpu.CompilerParams(
            dimension_semantics=("arbitrary", "arbitrary", "arbitrary"),
            vmem_limit_bytes=VMEM_LIMIT),
        name="diff_attn",
    )(q, k, v, lq1, lk1, lq2, lk2, sub)


def _ffn_tile(o_ref, i, nw_ref, wup_ref, cw_ref, cb_ref, wdn_ref, buf_ref, tm):
    hn = _rms(o_ref[0], nw_ref[...]).astype(BF16)
    half = D_FF // 2
    lead = SUBLANES - (FFN_CONV - 1)

    @pl.when(i == 0)
    def _():
        buf_ref[:, 0:SUBLANES, :] = jnp.zeros((4, SUBLANES, half), F32)

    for c in range(2):
        acts = []
        for part in range(2):
            col = part * D_FF + c * half
            slot = 2 * c + part
            buf_ref[slot, SUBLANES:SUBLANES + tm, :] = _dot(hn, wup_ref[:, col:col + half])
            taps = [cw_ref[k:k + 1, col:col + half] for k in range(FFN_CONV)]
            conv = _causal_conv(buf_ref.at[slot], tm, taps, lead) + cb_ref[:, col:col + half]
            buf_ref[slot, 0:SUBLANES, :] = buf_ref[slot, tm:tm + SUBLANES, :]
            acts.append(conv)
        act = (_silu(acts[0]) * acts[1]).astype(BF16)
        o_ref[0] += _dot(act, wdn_ref[c * half:(c + 1) * half, :])


def _out_ffn_kernel(h_ref, ys_ref, ya_ref, wo_ref, nw_ref, wup_ref, cw_ref, cb_ref, wdn_ref,
                    o_ref, buf_ref):
    i = pl.program_id(1)
    tm = h_ref.shape[1]
    o_ref[0] = (h_ref[0] + _dot(ys_ref[0], wo_ref[0:SSD_D_INNER, :])
                + _dot(ya_ref[0], wo_ref[SSD_D_INNER:SSD_D_INNER + DA_WIDTH, :]))
    _ffn_tile(o_ref, i, nw_ref, wup_ref, cw_ref, cb_ref, wdn_ref, buf_ref, tm)


def _out_ffn(h, ys, ya, wo, nw, wup, cw, cb, wdn, tm):
    B, L, D = h.shape
    grid = (B, L // tm)
    row = lambda w: pl.BlockSpec((1, tm, w), lambda b, i: (b, i, 0))
    return pl.pallas_call(
        _out_ffn_kernel, grid=grid,
        in_specs=[row(D), row(SSD_D_INNER), row(DA_WIDTH), _const_spec(wo.shape),
                  _const_spec(nw.shape), _const_spec(wup.shape), _const_spec(cw.shape),
                  _const_spec(cb.shape), _const_spec(wdn.shape)],
        out_specs=row(D),
        out_shape=jax.ShapeDtypeStruct((B, L, D), F32),
        scratch_shapes=[pltpu.VMEM((4, tm + SUBLANES, D_FF // 2), F32)],
        compiler_params=pltpu.CompilerParams(dimension_semantics=("arbitrary", "arbitrary"),
                                             vmem_limit_bytes=VMEM_LIMIT),
        name="out_ffn",
    )(h, ys, ya, wo, nw, wup, cw, cb, wdn)


def _sc_ffn_kernel(h_ref, mnw_ref, win_ref, scw_ref, wout_ref, nw_ref, wup_ref, cw_ref, cb_ref,
                   wdn_ref, o_ref, mbuf_ref, buf_ref):
    i = pl.program_id(1)
    tm = h_ref.shape[1]
    h = h_ref[0]
    hn = _rms(h, mnw_ref[...]).astype(BF16)

    @pl.when(i == 0)
    def _():
        mbuf_ref[0:SUBLANES, :] = jnp.zeros((SUBLANES, SC_WIDTH), F32)

    cg = _dot(hn, win_ref[:, SC_WIDTH:2 * SC_WIDTH])
    u = _dot(hn, win_ref[:, 2 * SC_WIDTH:3 * SC_WIDTH])
    mbuf_ref[SUBLANES:SUBLANES + tm, :] = cg * u
    taps = [scw_ref[k:k + 1, :] for k in range(SC_CONV)]
    conv = _causal_conv(mbuf_ref, tm, taps, SUBLANES - (SC_CONV - 1))
    mbuf_ref[0:SUBLANES, :] = mbuf_ref[tm:tm + SUBLANES, :]
    bg = _dot(hn, win_ref[:, 0:SC_WIDTH])
    o_ref[0] = h + _dot((bg * conv).astype(BF16), wout_ref[...])
    _ffn_tile(o_ref, i, nw_ref, wup_ref, cw_ref, cb_ref, wdn_ref, buf_ref, tm)


def _sc_ffn(h, mnw, win, scw, wout, nw, wup, cw, cb, wdn, tm):
    B, L, D = h.shape
    grid = (B, L // tm)
    row = pl.BlockSpec((1, tm, D), lambda b, i: (b, i, 0))
    return pl.pallas_call(
        _sc_ffn_kernel, grid=grid,
        in_specs=[row, _const_spec(mnw.shape), _const_spec(win.shape), _const_spec(scw.shape),
                  _const_spec(wout.shape), _const_spec(nw.shape), _const_spec(wup.shape),
                  _const_spec(cw.shape), _const_spec(cb.shape), _const_spec(wdn.shape)],
        out_specs=row,
        out_shape=jax.ShapeDtypeStruct((B, L, D), F32),
        scratch_shapes=[pltpu.VMEM((tm + SUBLANES, SC_WIDTH), F32),
                        pltpu.VMEM((4, tm + SUBLANES, D_FF // 2), F32)],
        compiler_params=pltpu.CompilerParams(dimension_semantics=("arbitrary", "arbitrary"),
                                             vmem_limit_bytes=VMEM_LIMIT),
        name="sc_ffn",
    )(h, mnw, win, scw, wout, nw, wup, cw, cb, wdn)


def _rope_table(seq):
    inv = 1.0 / (ROPE_THETA ** (jnp.arange(0, DA_HEAD_DIM, 2, dtype=F32) / DA_HEAD_DIM))
    ang = jnp.arange(seq, dtype=F32)[:, None] * inv[None, :]
    ang = jnp.concatenate([ang, ang], axis=-1)
    cos, sin = jnp.cos(ang), jnp.sin(ang)
    first = jnp.arange(DA_HEAD_DIM) < DA_HEAD_DIM // 2
    sin_a = jnp.where(first, -sin, 0.0)
    sin_b = jnp.where(first, 0.0, sin)
    rep = LANES // DA_HEAD_DIM
    return jnp.concatenate([jnp.tile(cos, (1, rep)), jnp.tile(sin_a, (1, rep)),
                            jnp.tile(sin_b, (1, rep))], axis=-1)


def _row(v):
    return v.reshape(1, -1).astype(F32)


def _pad_lanes(v):
    return jnp.pad(v.reshape(1, -1).astype(F32), ((0, 0), (0, LANES - v.shape[-1])))


def kernel(x, mix_norm, ffn_norm, hy_w_in, hy_conv_w, hy_conv_b, hy_dt_bias, hy_a_log, hy_d_skip,
           hy_ssd_norm, hy_q_norm, hy_k_norm, hy_lambda_q1, hy_lambda_k1, hy_lambda_q2,
           hy_lambda_k2, hy_subln, hy_w_out, sc_w_in, sc_conv_w, sc_w_out, ffn_w_up, ffn_conv_w,
           ffn_conv_b, ffn_w_down):
    B, L, D = x.shape
    tm = min(512, L)
    tq = min(512, L)

    w = hy_w_in[0]
    o_x = SSD_D_INNER
    o_dt = o_x + SSD_CONV_DIM
    o_q = o_dt + SSD_HEADS
    o_k = o_q + DA_WIDTH
    o_v = o_k + DA_WIDTH
    wz = w[:, 0:o_x].astype(BF16)
    wx = w[:, o_x:o_dt].astype(BF16)
    wdt = jnp.pad(w[:, o_dt:o_q], ((0, 0), (0, LANES - SSD_HEADS))).astype(BF16)
    wq = w[:, o_q:o_k].astype(BF16)
    wk = w[:, o_k:o_v].astype(BF16)
    wv = w[:, o_v:o_v + DA_WIDTH].astype(BF16)
    head_of = jnp.arange(DA_WIDTH) // DA_HEAD_DIM
    g = (head_of[:, None] == jnp.arange(LANES)[None, :]).astype(BF16)
    gt = g.T
    rope = _rope_table(L)
    z, xc, dtf, q, k, v = _even_in(
        x, _row(mix_norm[0]), wz, wx, wdt, wq, wk, wv, hy_conv_w[0].astype(F32),
        _row(hy_conv_b[0]), _pad_lanes(hy_dt_bias[0]),
        _row(jnp.tile(hy_q_norm[0], DA_WIDTH // DA_HEAD_DIM)),
        _row(jnp.tile(hy_k_norm[0], DA_WIDTH // DA_HEAD_DIM)), rope, g, gt, tm)
    y_ssd = _ssd(xc, dtf, z, _pad_lanes(hy_a_log[0]),
                 _row(jnp.repeat(hy_d_skip[0], SSD_HEAD_DIM)), _row(hy_ssd_norm[0]), tm)
    lambda_init = 0.8 - 0.6 * math.exp(-0.3 * 0)
    y_att = _diff_attn(q, k, v, _row(hy_lambda_q1[0]), _row(hy_lambda_k1[0]),
                       _row(hy_lambda_q2[0]), _row(hy_lambda_k2[0]), _row(hy_subln[0]),
                       lambda_init, tq)
    h = _out_ffn(x, y_ssd, y_att, hy_w_out[0].astype(BF16), _row(ffn_norm[0]),
                 ffn_w_up[0].astype(BF16), ffn_conv_w[0].astype(F32), _row(ffn_conv_b[0]),
                 ffn_w_down[0].astype(BF16), tm)

    h = _sc_ffn(h, _row(mix_norm[1]), sc_w_in[0].astype(BF16), sc_conv_w[0].astype(F32),
                sc_w_out[0].astype(BF16), _row(ffn_norm[1]), ffn_w_up[1].astype(BF16),
                ffn_conv_w[1].astype(F32), _row(ffn_conv_b[1]), ffn_w_down[1].astype(BF16), tm)
    return h
```

```python
import functools
import math

import jax
import jax.numpy as jnp
from jax import lax
from jax.experimental import pallas as pl
from jax.experimental.pallas import tpu as pltpu

F32 = jnp.float32
BF16 = jnp.bfloat16

RMS_EPS = 1e-6
ROPE_THETA = 10000.0
D_MODEL = 1024
SSD_HEAD_DIM = 64
SSD_HEADS = 16
SSD_GROUPS = 2
SSD_D_STATE = 128
SSD_D_INNER = 1024
SSD_CONV = 4
SSD_CHUNK = 128
SSD_CONV_DIM = SSD_D_INNER + 2 * SSD_GROUPS * SSD_D_STATE
DA_HEADS = 8
DA_HEAD_DIM = 64
DA_V_DIM = 128
DA_WIDTH = 1024
SC_WIDTH = 1024
SC_CONV = 3
D_FF = 2816
FFN_CONV = 3
LANES = 128
SUBLANES = 8
VMEM_LIMIT = 56 * 1024 * 1024
LOG2E = 1.4426950408889634
MAX_SAFE_SCORE_BOUND = 60.0


def _dot(a, b):
    return jnp.dot(a, b, preferred_element_type=F32)


def _split3(x):
    hi = x.astype(BF16)
    r1 = x - hi.astype(F32)
    mid = r1.astype(BF16)
    lo = (r1 - mid.astype(F32)).astype(BF16)
    return hi, mid, lo


def _dot_sel_rhs(x, sel):
    hi, mid, lo = _split3(x)
    return _dot(hi, sel) + _dot(mid, sel) + _dot(lo, sel)


def _dot_sel_rhs16(x, sel2):
    hi = x.astype(BF16)
    mid = (x - hi.astype(F32)).astype(BF16)
    return _dot(jnp.concatenate([hi, mid], axis=1), sel2)


def _dot_sel_lhs(sel, x):
    hi, mid, lo = _split3(x)
    return _dot(sel, hi) + _dot(sel, mid) + _dot(sel, lo)


def _rms(x, w):
    return x * lax.rsqrt(jnp.mean(x * x, axis=-1, keepdims=True) + RMS_EPS) * w


def _silu(x):
    return x * (1.0 / (1.0 + jnp.exp(-x)))


def _softplus(x):
    return jnp.maximum(x, 0.0) + jnp.log1p(jnp.exp(-jnp.abs(x)))


def _const_spec(shape):
    nd = len(shape)
    return pl.BlockSpec(shape, lambda *_: (0,) * nd, pipeline_mode=pl.Buffered(1))


def _causal_conv(buf_ref, tm, taps, first_row):
    acc = None
    for k, w in enumerate(taps):
        term = buf_ref[pl.ds(first_row + k, tm), :] * w
        acc = term if acc is None else acc + term
    return acc


def _even_in_kernel(h_ref, nw_ref, wz_ref, wx_ref, wdt_ref, wq_ref, wk_ref, wv_ref,
                    cw_ref, cb_ref, dtb_ref, qn_ref, kn_ref, rope_ref, g_ref, gt_ref,
                    z_ref, xc_ref, dt_ref, q_ref, k_ref, v_ref, xs_ref):
    i = pl.program_id(1)
    tm = h_ref.shape[1]
    hn = _rms(h_ref[0], nw_ref[...]).astype(BF16)

    z_ref[0] = _dot(hn, wz_ref[...]).astype(BF16)
    v_ref[0] = _dot(hn, wv_ref[...]).astype(BF16)
    dt_ref[0] = _softplus(_dot(hn, wdt_ref[...]) + dtb_ref[...])

    @pl.when(i == 0)
    def _():
        xs_ref[0:SUBLANES, :] = jnp.zeros((SUBLANES, SSD_CONV_DIM), F32)

    xs_ref[SUBLANES:SUBLANES + tm, :] = _dot(hn, wx_ref[...])
    taps = [cw_ref[k:k + 1, :] for k in range(SSD_CONV)]
    conv = _causal_conv(xs_ref, tm, taps, SUBLANES - (SSD_CONV - 1)) + cb_ref[...]
    xc_ref[0] = _silu(conv).astype(BF16)
    xs_ref[0:SUBLANES, :] = xs_ref[tm:tm + SUBLANES, :]

    cos = jnp.tile(rope_ref[:, 0:LANES], (1, DA_WIDTH // LANES))
    sin_a = jnp.tile(rope_ref[:, LANES:2 * LANES], (1, DA_WIDTH // LANES))
    sin_b = jnp.tile(rope_ref[:, 2 * LANES:3 * LANES], (1, DA_WIDTH // LANES))
    half = DA_HEAD_DIM // 2
    for w_ref, n_ref, o_ref, scale in ((wq_ref, qn_ref, q_ref, LOG2E * DA_HEAD_DIM ** -0.5),
                                       (wk_ref, kn_ref, k_ref, 1.0)):
        x = _dot(hn, w_ref[...])
        ss = _dot((x * x).astype(BF16), g_ref[...])
        r = lax.rsqrt(ss * (1.0 / DA_HEAD_DIM) + RMS_EPS) * scale
        xn = x * _dot_sel_rhs(r, gt_ref[...]) * n_ref[...]
        rot = (pltpu.roll(xn, DA_WIDTH - half, 1) * sin_a + pltpu.roll(xn, half, 1) * sin_b)
        o_ref[0] = (xn * cos + rot).astype(BF16)


def _even_in(h, nw, wz, wx, wdt, wq, wk, wv, cw, cb, dtb, qn, kn, rope, g, gt, tm):
    B, L, D = h.shape
    grid = (B, L // tm)
    row = lambda w: pl.BlockSpec((1, tm, w), lambda b, i: (b, i, 0))
    out_shape = (jax.ShapeDtypeStruct((B, L, SSD_D_INNER), BF16),
                 jax.ShapeDtypeStruct((B, L, SSD_CONV_DIM), BF16),
                 jax.ShapeDtypeStruct((B, L, LANES), F32),
                 jax.ShapeDtypeStruct((B, L, DA_WIDTH), BF16),
                 jax.ShapeDtypeStruct((B, L, DA_WIDTH), BF16),
                 jax.ShapeDtypeStruct((B, L, DA_WIDTH), BF16))
    in_specs = [row(D), _const_spec(nw.shape), _const_spec(wz.shape), _const_spec(wx.shape),
                _const_spec(wdt.shape), _const_spec(wq.shape), _const_spec(wk.shape),
                _const_spec(wv.shape), _const_spec(cw.shape), _const_spec(cb.shape),
                _const_spec(dtb.shape), _const_spec(qn.shape), _const_spec(kn.shape),
                pl.BlockSpec((tm, 3 * LANES), lambda b, i: (i, 0)),
                _const_spec(g.shape), _const_spec(gt.shape)]
    out_specs = (row(SSD_D_INNER), row(SSD_CONV_DIM), row(LANES), row(DA_WIDTH), row(DA_WIDTH),
                 row(DA_WIDTH))
    return pl.pallas_call(
        _even_in_kernel, grid=grid, in_specs=in_specs, out_specs=out_specs, out_shape=out_shape,
        scratch_shapes=[pltpu.VMEM((tm + SUBLANES, SSD_CONV_DIM), F32)],
        compiler_params=pltpu.CompilerParams(dimension_semantics=("arbitrary", "arbitrary"),
                                             vmem_limit_bytes=VMEM_LIMIT),
        name="even_in",
    )(h, nw, wz, wx, wdt, wq, wk, wv, cw, cb, dtb, qn, kn, rope, g, gt)


def _ssd_kernel(xc_ref, dt_ref, z_ref, alog_ref, dskip_ref, nw_ref, tril_ref, expand_ref,
                y_ref, state_ref):
    T = SSD_CHUNK
    GN = SSD_GROUPS * SSD_D_STATE
    GW = SSD_D_INNER // SSD_GROUPS
    heads_per_group = SSD_HEADS // SSD_GROUPS

    @pl.when(pl.program_id(1) == 0)
    def _():
        state_ref[...] = jnp.zeros(state_ref.shape, F32)

    a = -jnp.exp(alog_ref[...])
    r_i = lax.broadcasted_iota(jnp.int32, (T, T), 0)
    c_i = lax.broadcasted_iota(jnp.int32, (T, T), 1)
    tril = r_i >= c_i
    lane = lax.broadcasted_iota(jnp.int32, (T, LANES), 1)

    for ci in range(xc_ref.shape[1] // T):
        rows = slice(ci * T, (ci + 1) * T)
        x = xc_ref[0, rows, 0:SSD_D_INNER].astype(F32)
        bm = xc_ref[0, rows, SSD_D_INNER:SSD_D_INNER + GN]
        cm = xc_ref[0, rows, SSD_D_INNER + GN:SSD_D_INNER + 2 * GN]
        dt = dt_ref[0, rows, :]
        a_cs = _dot_sel_lhs(tril_ref[...], dt * a)
        a_cs_t = a_cs.T
        a_last = a_cs[T - 1:T, :]

        stacked = jnp.concatenate(
            [dt, jnp.exp(a_cs), jnp.exp(a_last - a_cs),
             jnp.broadcast_to(jnp.exp(a_last), (SUBLANES, LANES))], axis=0)
        ex = _dot_sel_rhs16(stacked, expand_ref[...])
        dt_e, expa_e, decay_e = ex[0:T], ex[T:2 * T], ex[2 * T:3 * T]
        chunk_decay = ex[3 * T:3 * T + 1]
        xdt = x * dt_e

        xdt_b = xdt.astype(BF16)
        y_parts = []
        for g in range(SSD_GROUPS):
            cg = cm[:, g * SSD_D_STATE:(g + 1) * SSD_D_STATE]
            bg = bm[:, g * SSD_D_STATE:(g + 1) * SSD_D_STATE]
            cb = lax.dot_general(cg, bg, (((1,), (1,)), ((), ())), preferred_element_type=F32)
            for hp in range(heads_per_group // 2):
                h0 = g * heads_per_group + 2 * hp
                ms = []
                for h in (h0, h0 + 1):
                    seg = a_cs[:, h:h + 1] - a_cs_t[h:h + 1, :]
                    ms.append((cb * jnp.exp(jnp.where(tril, seg, -jnp.inf))).astype(BF16))
                lhs = jnp.concatenate(ms, axis=1)
                xp = xdt_b[:, h0 * SSD_HEAD_DIM:(h0 + 2) * SSD_HEAD_DIM]
                rhs = jnp.concatenate([jnp.where(lane < SSD_HEAD_DIM, xp, 0),
                                       jnp.where(lane >= SSD_HEAD_DIM, xp, 0)], axis=0)
                y_parts.append(_dot(lhs, rhs))
        y_diag = jnp.concatenate(y_parts, axis=1)

        xdec = (xdt * decay_e).astype(BF16)
        y_off_parts = []
        for g in range(SSD_GROUPS):
            cg = cm[:, g * SSD_D_STATE:(g + 1) * SSD_D_STATE]
            y_off_parts.append(_dot(cg, state_ref[g].astype(BF16)))
            bg_t = bm[:, g * SSD_D_STATE:(g + 1) * SSD_D_STATE].astype(F32).T.astype(BF16)
            upd = _dot(bg_t, xdec[:, g * GW:(g + 1) * GW])
            state_ref[g] = state_ref[g] * chunk_decay[:, g * GW:(g + 1) * GW] + upd
        y_off = jnp.concatenate(y_off_parts, axis=1) * expa_e

        y = y_diag + y_off + dskip_ref[...] * x
        y = y * _silu(z_ref[0, rows, :].astype(F32))
        outs = []
        for g in range(SSD_GROUPS):
            yg = y[:, g * GW:(g + 1) * GW]
            outs.append(yg * lax.rsqrt(jnp.mean(yg * yg, axis=-1, keepdims=True) + RMS_EPS))
        y_ref[0, rows, :] = (jnp.concatenate(outs, axis=1) * nw_ref[...]).astype(BF16)


def _ssd(xc, dtf, z, alog, dskip, nw, ts):
    B, L, _ = xc.shape
    T = SSD_CHUNK
    grid = (B, L // ts)
    row = lambda w: pl.BlockSpec((1, ts, w), lambda b, c: (b, c, 0))
    tril = (jnp.arange(T)[:, None] >= jnp.arange(T)[None, :]).astype(BF16)
    expand = (jnp.arange(SSD_D_INNER)[None, :] // SSD_HEAD_DIM
              == jnp.arange(2 * LANES)[:, None] % LANES).astype(BF16)
    return pl.pallas_call(
        _ssd_kernel, grid=grid,
        in_specs=[row(SSD_CONV_DIM), row(LANES), row(SSD_D_INNER),
                  _const_spec(alog.shape), _const_spec(dskip.shape), _const_spec(nw.shape),
                  _const_spec(tril.shape), _const_spec(expand.shape)],
        out_specs=row(SSD_D_INNER),
        out_shape=jax.ShapeDtypeStruct((B, L, SSD_D_INNER), BF16),
        scratch_shapes=[pltpu.VMEM((SSD_GROUPS, SSD_D_STATE, SSD_D_INNER // SSD_GROUPS), F32)],
        compiler_params=pltpu.CompilerParams(dimension_semantics=("arbitrary", "arbitrary"),
                                             vmem_limit_bytes=VMEM_LIMIT),
        name="ssd",
    )(xc, dtf, z, alog, dskip, nw, tril, expand)


def _attn_kernel(q_ref, k_ref, v_ref, lq1_ref, lk1_ref, lq2_ref, lk2_ref, sub_ref, o_ref, q2_ref,
                 kmax_ref, *, tq, lambda_init):
    i = pl.program_id(2)
    tk = tq // 2
    q = q_ref[0]
    lane = lax.broadcasted_iota(jnp.int32, q.shape, 1)
    zero = jnp.zeros_like(q)
    q_map1 = jnp.where(lane < DA_HEAD_DIM, q, zero)
    q_map2 = jnp.where(lane >= DA_HEAD_DIM, q, zero)
    q2_ref[0:tk, :] = q_map1[0:tk]
    q2_ref[tk:2 * tk, :] = q_map2[0:tk]
    q2_ref[2 * tk:3 * tk, :] = q_map1[tk:tq]
    q2_ref[3 * tk:4 * tk, :] = q_map2[tk:tq]
    late = slice(2 * tk, 4 * tk)

    def scores(j, late_only=False):
        kb = k_ref[0, pl.ds(pl.multiple_of(j * tk, tk), tk), :]
        qs = q2_ref[late, :] if late_only else q2_ref[...]
        return lax.dot_general(kb, qs, (((1,), (1,)), ((), ())), preferred_element_type=F32)

    def causal(s):
        kpos = lax.broadcasted_iota(jnp.int32, s.shape, 0)
        col = lax.broadcasted_iota(jnp.int32, s.shape, 1)
        qpos = jnp.where(col >= tk, col - tk, col)
        return jnp.where(kpos <= qpos, s, -jnp.inf)

    def diagonal_scores():
        s_a = scores(2 * i)
        s_a = jnp.concatenate([causal(s_a[:, 0:2 * tk]), s_a[:, late]], axis=1)
        return s_a, causal(scores(2 * i + 1, late_only=True))

    def v_block(j):
        return v_ref[0, pl.ds(pl.multiple_of(j * tk, tk), tk), :]

    def update(j, s, carry):
        m, l, acc = carry
        m_new = jnp.maximum(m, jnp.max(s, axis=0, keepdims=True))
        p = jnp.exp2(s - m_new)
        alpha = jnp.exp2(m - m_new)
        l_new = alpha * l + jnp.sum(p, axis=0, keepdims=True)
        pv = lax.dot_general(v_block(j), p.astype(BF16), (((0,), (0,)), ((), ())),
                             preferred_element_type=F32)
        return m_new, l_new, alpha * acc + pv

    def pair(j0, carry):
        s_a = scores(j0)
        s_b = scores(j0 + 1)
        return update(j0 + 1, s_b, update(j0, s_a, carry))

    def online_path():
        init = (jnp.full((1, 2 * tq), -jnp.inf, F32), jnp.zeros((1, 2 * tq), F32),
                jnp.zeros((DA_V_DIM, 2 * tq), F32))
        carry = lax.fori_loop(0, i, lambda t, c: pair(2 * t, c), init)
        s_a, s_b = diagonal_scores()
        m, l, acc = update(2 * i, s_a, carry)
        _, l_late, acc_late = update(2 * i + 1, s_b, (m[:, late], l[:, late], acc[:, late]))
        return (jnp.concatenate([l[:, 0:2 * tk], l_late], axis=1),
                jnp.concatenate([acc[:, 0:2 * tk], acc_late], axis=1))

    @pl.when(i == 0)
    def _():
        kf = k_ref[0].astype(F32)
        r_i = lax.broadcasted_iota(jnp.int32, (DA_V_DIM, LANES), 0)
        c_i = lax.broadcasted_iota(jnp.int32, (DA_V_DIM, LANES), 1)
        map_sel = jnp.where(r_i // DA_HEAD_DIM == c_i, 1.0, 0.0).astype(BF16)
        kn = _dot((kf * kf).astype(BF16), map_sel)
        kmax_ref[...] = jnp.max(kn, axis=0, keepdims=True)

    q2f = q2_ref[...].astype(F32)
    qn = lax.dot_general(jnp.ones((SUBLANES, DA_V_DIM), BF16), (q2f * q2f).astype(BF16),
                         (((1,), (1,)), ((), ())), preferred_element_type=F32)[0:1]
    qcol = lax.broadcasted_iota(jnp.int32, (1, 2 * tq), 1)
    kmax = jnp.where((qcol // tk) % 2 == 0, kmax_ref[:, 0:1], kmax_ref[:, 1:2])
    bound = jnp.sqrt(qn * kmax)

    def bounded_update(j, s, shift, carry):
        l, acc = carry
        p = jnp.exp2(s - shift)
        pv = lax.dot_general(v_block(j), p.astype(BF16), (((0,), (0,)), ((), ())),
                             preferred_element_type=F32)
        return l + jnp.sum(p, axis=0, keepdims=True), acc + pv

    def bounded_pair(j0, carry):
        carry = bounded_update(j0, scores(j0), bound, carry)
        return bounded_update(j0 + 1, scores(j0 + 1), bound, carry)

    def bounded_path():
        init = (jnp.zeros((1, 2 * tq), F32), jnp.zeros((DA_V_DIM, 2 * tq), F32))
        carry = lax.fori_loop(0, i, lambda t, c: bounded_pair(2 * t, c), init)
        s_a, s_b = diagonal_scores()
        l, acc = bounded_update(2 * i, s_a, bound, carry)
        l_late, acc_late = bounded_update(2 * i + 1, s_b, bound[:, late], (l[:, late], acc[:, late]))
        return (jnp.concatenate([l[:, 0:2 * tk], l_late], axis=1),
                jnp.concatenate([acc[:, 0:2 * tk], acc_late], axis=1))

    l, acc = lax.cond(jnp.max(bound) <= MAX_SAFE_SCORE_BOUND, bounded_path, online_path)
    o_t = acc / l

    lam = (jnp.exp(jnp.sum(lq1_ref[...] * lk1_ref[...], axis=-1, keepdims=True))
           - jnp.exp(jnp.sum(lq2_ref[...] * lk2_ref[...], axis=-1, keepdims=True)) + lambda_init)
    o_map1 = jnp.concatenate([o_t[:, 0:tk], o_t[:, 2 * tk:3 * tk]], axis=1)
    o_map2 = jnp.concatenate([o_t[:, tk:2 * tk], o_t[:, 3 * tk:4 * tk]], axis=1)
    o = (o_map1 - lam * o_map2).T
    o = _rms(o, sub_ref[...]) * (1.0 - lambda_init)
    o_ref[0] = o.astype(BF16)


def _diff_attn(q, k, v, lq1, lk1, lq2, lk2, sub, lambda_init, tq):
    B, L, _ = q.shape
    grid = (B, DA_HEADS, L // tq)
    full = pl.BlockSpec((1, L, DA_V_DIM), lambda b, h, i: (b, 0, h))
    blk = pl.BlockSpec((1, tq, DA_V_DIM), lambda b, h, i: (b, i, h))
    return pl.pallas_call(
        functools.partial(_attn_kernel, tq=tq, lambda_init=lambda_init),
        grid=grid,
        scratch_shapes=[pltpu.VMEM((2 * tq, DA_V_DIM), BF16), pltpu.VMEM((1, LANES), F32)],
        in_specs=[blk, full, full, _const_spec(lq1.shape), _const_spec(lk1.shape),
                  _const_spec(lq2.shape), _const_spec(lk2.shape), _const_spec(sub.shape)],
        out_specs=blk,
        out_shape=jax.ShapeDtypeStruct((B, L, DA_WIDTH), BF16),
        compiler_params=pltpu.CompilerParams(
            dimension_semantics=("arbitrary", "arbitrary", "arbitrary"),
            vmem_limit_bytes=VMEM_LIMIT),
        name="diff_attn",
    )(q, k, v, lq1, lk1, lq2, lk2, sub)


def _ffn_tile(o_ref, i, nw_ref, wup_ref, cw_ref, cb_ref, wdn_ref, buf_ref, tm):
    hn = _rms(o_ref[0], nw_ref[...]).astype(BF16)
    half = D_FF // 2
    lead = SUBLANES - (FFN_CONV - 1)

    @pl.when(i == 0)
    def _():
        buf_ref[:, 0:SUBLANES, :] = jnp.zeros((4, SUBLANES, half), F32)

    for c in range(2):
        acts = []
        for part in range(2):
            col = part * D_FF + c * half
            slot = 2 * c + part
            buf_ref[slot, SUBLANES:SUBLANES + tm, :] = _dot(hn, wup_ref[:, col:col + half])
            taps = [cw_ref[k:k + 1, col:col + half] for k in range(FFN_CONV)]
            conv = _causal_conv(buf_ref.at[slot], tm, taps, lead) + cb_ref[:, col:col + half]
            buf_ref[slot, 0:SUBLANES, :] = buf_ref[slot, tm:tm + SUBLANES, :]
            acts.append(conv)
        act = (_silu(acts[0]) * acts[1]).astype(BF16)
        o_ref[0] += _dot(act, wdn_ref[c * half:(c + 1) * half, :])


def _out_ffn_kernel(h_ref, ys_ref, ya_ref, wo_ref, nw_ref, wup_ref, cw_ref, cb_ref, wdn_ref,
                    o_ref, buf_ref):
    i = pl.program_id(1)
    tm = h_ref.shape[1]
    o_ref[0] = (h_ref[0] + _dot(ys_ref[0], wo_ref[0:SSD_D_INNER, :])
                + _dot(ya_ref[0], wo_ref[SSD_D_INNER:SSD_D_INNER + DA_WIDTH, :]))
    _ffn_tile(o_ref, i, nw_ref, wup_ref, cw_ref, cb_ref, wdn_ref, buf_ref, tm)


def _out_ffn(h, ys, ya, wo, nw, wup, cw, cb, wdn, tm):
    B, L, D = h.shape
    grid = (B, L // tm)
    row = lambda w: pl.BlockSpec((1, tm, w), lambda b, i: (b, i, 0))
    return pl.pallas_call(
        _out_ffn_kernel, grid=grid,
        in_specs=[row(D), row(SSD_D_INNER), row(DA_WIDTH), _const_spec(wo.shape),
                  _const_spec(nw.shape), _const_spec(wup.shape), _const_spec(cw.shape),
                  _const_spec(cb.shape), _const_spec(wdn.shape)],
        out_specs=row(D),
        out_shape=jax.ShapeDtypeStruct((B, L, D), F32),
        scratch_shapes=[pltpu.VMEM((4, tm + SUBLANES, D_FF // 2), F32)],
        compiler_params=pltpu.CompilerParams(dimension_semantics=("arbitrary", "arbitrary"),
                                             vmem_limit_bytes=VMEM_LIMIT),
        name="out_ffn",
    )(h, ys, ya, wo, nw, wup, cw, cb, wdn)


def _sc_ffn_kernel(h_ref, mnw_ref, win_ref, scw_ref, wout_ref, nw_ref, wup_ref, cw_ref, cb_ref,
                   wdn_ref, o_ref, mbuf_ref, buf_ref):
    i = pl.program_id(1)
    tm = h_ref.shape[1]
    h = h_ref[0]
    hn = _rms(h, mnw_ref[...]).astype(BF16)

    @pl.when(i == 0)
    def _():
        mbuf_ref[0:SUBLANES, :] = jnp.zeros((SUBLANES, SC_WIDTH), F32)

    cg = _dot(hn, win_ref[:, SC_WIDTH:2 * SC_WIDTH])
    u = _dot(hn, win_ref[:, 2 * SC_WIDTH:3 * SC_WIDTH])
    mbuf_ref[SUBLANES:SUBLANES + tm, :] = cg * u
    taps = [scw_ref[k:k + 1, :] for k in range(SC_CONV)]
    conv = _causal_conv(mbuf_ref, tm, taps, SUBLANES - (SC_CONV - 1))
    mbuf_ref[0:SUBLANES, :] = mbuf_ref[tm:tm + SUBLANES, :]
    bg = _dot(hn, win_ref[:, 0:SC_WIDTH])
    o_ref[0] = h + _dot((bg * conv).astype(BF16), wout_ref[...])
    _ffn_tile(o_ref, i, nw_ref, wup_ref, cw_ref, cb_ref, wdn_ref, buf_ref, tm)


def _sc_ffn(h, mnw, win, scw, wout, nw, wup, cw, cb, wdn, tm):
    B, L, D = h.shape
    grid = (B, L // tm)
    row = pl.BlockSpec((1, tm, D), lambda b, i: (b, i, 0))
    return pl.pallas_call(
        _sc_ffn_kernel, grid=grid,
        in_specs=[row, _const_spec(mnw.shape), _const_spec(win.shape), _const_spec(scw.shape),
                  _const_spec(wout.shape), _const_spec(nw.shape), _const_spec(wup.shape),
                  _const_spec(cw.shape), _const_spec(cb.shape), _const_spec(wdn.shape)],
        out_specs=row,
        out_shape=jax.ShapeDtypeStruct((B, L, D), F32),
        scratch_shapes=[pltpu.VMEM((tm + SUBLANES, SC_WIDTH), F32),
                        pltpu.VMEM((4, tm + SUBLANES, D_FF // 2), F32)],
        compiler_params=pltpu.CompilerParams(dimension_semantics=("arbitrary", "arbitrary"),
                                             vmem_limit_bytes=VMEM_LIMIT),
        name="sc_ffn",
    )(h, mnw, win, scw, wout, nw, wup, cw, cb, wdn)


def _rope_table(seq):
    inv = 1.0 / (ROPE_THETA ** (jnp.arange(0, DA_HEAD_DIM, 2, dtype=F32) / DA_HEAD_DIM))
    ang = jnp.arange(seq, dtype=F32)[:, None] * inv[None, :]
    ang = jnp.concatenate([ang, ang], axis=-1)
    cos, sin = jnp.cos(ang), jnp.sin(ang)
    first = jnp.arange(DA_HEAD_DIM) < DA_HEAD_DIM // 2
    sin_a = jnp.where(first, -sin, 0.0)
    sin_b = jnp.where(first, 0.0, sin)
    rep = LANES // DA_HEAD_DIM
    return jnp.concatenate([jnp.tile(cos, (1, rep)), jnp.tile(sin_a, (1, rep)),
                            jnp.tile(sin_b, (1, rep))], axis=-1)


def _row(v):
    return v.reshape(1, -1).astype(F32)


def _pad_lanes(v):
    return jnp.pad(v.reshape(1, -1).astype(F32), ((0, 0), (0, LANES - v.shape[-1])))


def kernel(x, mix_norm, ffn_norm, hy_w_in, hy_conv_w, hy_conv_b, hy_dt_bias, hy_a_log, hy_d_skip,
           hy_ssd_norm, hy_q_norm, hy_k_norm, hy_lambda_q1, hy_lambda_k1, hy_lambda_q2,
           hy_lambda_k2, hy_subln, hy_w_out, sc_w_in, sc_conv_w, sc_w_out, ffn_w_up, ffn_conv_w,
           ffn_conv_b, ffn_w_down):
    B, L, D = x.shape
    tm = min(512, L)
    tq = min(1024, L)

    w = hy_w_in[0]
    o_x = SSD_D_INNER
    o_dt = o_x + SSD_CONV_DIM
    o_q = o_dt + SSD_HEADS
    o_k = o_q + DA_WIDTH
    o_v = o_k + DA_WIDTH
    wz = w[:, 0:o_x].astype(BF16)
    wx = w[:, o_x:o_dt].astype(BF16)
    wdt = jnp.pad(w[:, o_dt:o_q], ((0, 0), (0, LANES - SSD_HEADS))).astype(BF16)
    wq = w[:, o_q:o_k].astype(BF16)
    wk = w[:, o_k:o_v].astype(BF16)
    wv = w[:, o_v:o_v + DA_WIDTH].astype(BF16)
    head_of = jnp.arange(DA_WIDTH) // DA_HEAD_DIM
    g = (head_of[:, None] == jnp.arange(LANES)[None, :]).astype(BF16)
    gt = g.T
    rope = _rope_table(L)
    z, xc, dtf, q, k, v = _even_in(
        x, _row(mix_norm[0]), wz, wx, wdt, wq, wk, wv, hy_conv_w[0].astype(F32),
        _row(hy_conv_b[0]), _pad_lanes(hy_dt_bias[0]),
        _row(jnp.tile(hy_q_norm[0], DA_WIDTH // DA_HEAD_DIM)),
        _row(jnp.tile(hy_k_norm[0], DA_WIDTH // DA_HEAD_DIM)), rope, g, gt, tm)
    y_ssd = _ssd(xc, dtf, z, _pad_lanes(hy_a_log[0]),
                 _row(jnp.repeat(hy_d_skip[0], SSD_HEAD_DIM)), _row(hy_ssd_norm[0]), tm)
    lambda_init = 0.8 - 0.6 * math.exp(-0.3 * 0)
    y_att = _diff_attn(q, k, v, _row(hy_lambda_q1[0]), _row(hy_lambda_k1[0]),
                       _row(hy_lambda_q2[0]), _row(hy_lambda_k2[0]), _row(hy_subln[0]),
                       lambda_init, tq)
    h = _out_ffn(x, y_ssd, y_att, hy_w_out[0].astype(BF16), _row(ffn_norm[0]),
                 ffn_w_up[0].astype(BF16), ffn_conv_w[0].astype(F32), _row(ffn_conv_b[0]),
                 ffn_w_down[0].astype(BF16), tm)

    h = _sc_ffn(h, _row(mix_norm[1]), sc_w_in[0].astype(BF16), sc_conv_w[0].astype(F32),
                sc_w_out[0].astype(BF16), _row(ffn_norm[1]), ffn_w_up[1].astype(BF16),
                ffn_conv_w[1].astype(F32), _row(ffn_conv_b[1]), ffn_w_down[1].astype(BF16), tm)
    return h
```

```python
import functools
import math

import jax
import jax.numpy as jnp
from jax import lax
from jax.experimental import pallas as pl
from jax.experimental.pallas import tpu as pltpu

F32 = jnp.float32
BF16 = jnp.bfloat16

RMS_EPS = 1e-6
ROPE_THETA = 10000.0
D_MODEL = 1024
SSD_HEAD_DIM = 64
SSD_HEADS = 16
SSD_GROUPS = 2
SSD_D_STATE = 128
SSD_D_INNER = 1024
SSD_CONV = 4
SSD_CHUNK = 128
SSD_CONV_DIM = SSD_D_INNER + 2 * SSD_GROUPS * SSD_D_STATE
DA_HEADS = 8
DA_HEAD_DIM = 64
HEADS_QK = 2 * DA_HEADS
DA_V_DIM = 128
DA_WIDTH = 1024
SC_WIDTH = 1024
SC_CONV = 3
D_FF = 2816
FFN_CONV = 3
FFN_CHUNKS = (768, 768, 768, 512)
LANES = 128
SUBLANES = 8
VMEM_LIMIT = 56 * 1024 * 1024
LOG2E = 1.4426950408889634
MAX_SAFE_SCORE_BOUND = 60.0


def _dot(a, b):
    return jnp.dot(a, b, preferred_element_type=F32)


def _split3(x):
    hi = x.astype(BF16)
    r1 = x - hi.astype(F32)
    mid = r1.astype(BF16)
    lo = (r1 - mid.astype(F32)).astype(BF16)
    return hi, mid, lo


def _dot_sel_rhs(x, sel):
    hi, mid, lo = _split3(x)
    return _dot(hi, sel) + _dot(mid, sel) + _dot(lo, sel)


def _dot_sel_rhs16(x, sel2):
    hi = x.astype(BF16)
    mid = (x - hi.astype(F32)).astype(BF16)
    return _dot(jnp.concatenate([hi, mid], axis=1), sel2)


def _dot_sel_lhs(sel, x):
    hi, mid, lo = _split3(x)
    return _dot(sel, hi) + _dot(sel, mid) + _dot(sel, lo)


def _rms(x, w):
    return x * lax.rsqrt(jnp.mean(x * x, axis=-1, keepdims=True) + RMS_EPS) * w


def _silu(x):
    return x * (1.0 / (1.0 + jnp.exp(-x)))


def _softplus(x):
    return jnp.maximum(x, 0.0) + jnp.log1p(jnp.exp(-jnp.abs(x)))


def _const_spec(shape):
    nd = len(shape)
    return pl.BlockSpec(shape, lambda *_: (0,) * nd, pipeline_mode=pl.Buffered(1))


def _causal_conv(buf_ref, tm, taps, first_row):
    xe = buf_ref[...]
    acc = None
    for k, w in enumerate(taps):
        back = SUBLANES - (first_row + k)
        shifted = xe if back == 0 else pltpu.roll(xe, back, 0)
        term = shifted[SUBLANES:SUBLANES + tm] * w
        acc = term if acc is None else acc + term
    return acc


def _even_in_kernel(h_ref, nw_ref, wz_ref, wx_ref, wdt_ref, wq_ref, wk_ref, wv_ref,
                    cw_ref, cb_ref, dtb_ref, qn_ref, kn_ref, rope_ref, gq_ref, gk_ref, gt_ref,
                    z_ref, xc_ref, dt_ref, q_ref, k_ref, v_ref, qsq_ref, ksq_ref, xs_ref):
    i = pl.program_id(1)
    tm = h_ref.shape[1]
    hn = _rms(h_ref[0], nw_ref[...]).astype(BF16)

    @pl.when(i == 0)
    def _():
        xs_ref[0:SUBLANES, :] = jnp.zeros((SUBLANES, SSD_CONV_DIM), F32)

    xs_ref[SUBLANES:SUBLANES + tm, :] = _dot(hn, wx_ref[...])
    q_raw = _dot(hn, wq_ref[...])

    taps = [cw_ref[k:k + 1, :] for k in range(SSD_CONV)]
    conv = _causal_conv(xs_ref, tm, taps, SUBLANES - (SSD_CONV - 1)) + cb_ref[...]
    xc_ref[0] = _silu(conv).astype(BF16)
    xs_ref[0:SUBLANES, :] = xs_ref[tm:tm + SUBLANES, :]

    k_raw = _dot(hn, wk_ref[...])

    cos = jnp.tile(rope_ref[:, 0:LANES], (1, DA_WIDTH // LANES))
    sin_a = jnp.tile(rope_ref[:, LANES:2 * LANES], (1, DA_WIDTH // LANES))
    sin_b = jnp.tile(rope_ref[:, 2 * LANES:3 * LANES], (1, DA_WIDTH // LANES))
    half = DA_HEAD_DIM // 2

    def qk_epilogue(x, n_ref, g_ref, o_ref, sq_ref, scale):
        ss = _dot((x * x).astype(BF16), g_ref[...])
        r = lax.rsqrt(ss * (1.0 / DA_HEAD_DIM) + RMS_EPS) * scale
        xn = x * _dot_sel_rhs16(r, gt_ref[...]) * n_ref[...]
        rot = (pltpu.roll(xn, DA_WIDTH - half, 1) * sin_a + pltpu.roll(xn, half, 1) * sin_b)
        o_ref[0] = (xn * cos + rot).astype(BF16)
        nsq = r * r * pltpu.roll(ss, LANES - HEADS_QK, 1)
        sq_ref[0] = nsq.T[0:HEADS_QK, :]

    qk_epilogue(q_raw, qn_ref, gq_ref, q_ref, qsq_ref, LOG2E * DA_HEAD_DIM ** -0.5)
    z_ref[0] = _dot(hn, wz_ref[...]).astype(BF16)
    qk_epilogue(k_raw, kn_ref, gk_ref, k_ref, ksq_ref, 1.0)
    v_ref[0] = _dot(hn, wv_ref[...]).astype(BF16)
    dt_ref[0] = _softplus(_dot(hn, wdt_ref[...]) + dtb_ref[...])


def _even_in(h, nw, wz, wx, wdt, wq, wk, wv, cw, cb, dtb, qn, kn, rope, gq, gk, gt, tm):
    B, L, D = h.shape
    grid = (B, L // tm)
    row = lambda w: pl.BlockSpec((1, tm, w), lambda b, i: (b, i, 0))
    out_shape = (jax.ShapeDtypeStruct((B, L, SSD_D_INNER), BF16),
                 jax.ShapeDtypeStruct((B, L, SSD_CONV_DIM), BF16),
                 jax.ShapeDtypeStruct((B, L, LANES), F32),
                 jax.ShapeDtypeStruct((B, L, DA_WIDTH), BF16),
                 jax.ShapeDtypeStruct((B, L, DA_WIDTH), BF16),
                 jax.ShapeDtypeStruct((B, L, DA_WIDTH), BF16),
                 jax.ShapeDtypeStruct((B, HEADS_QK, L), F32),
                 jax.ShapeDtypeStruct((B, HEADS_QK, L), F32))
    head_rows = pl.BlockSpec((1, HEADS_QK, tm), lambda b, i: (b, 0, i))
    in_specs = [row(D), _const_spec(nw.shape), _const_spec(wz.shape), _const_spec(wx.shape),
                _const_spec(wdt.shape), _const_spec(wq.shape), _const_spec(wk.shape),
                _const_spec(wv.shape), _const_spec(cw.shape), _const_spec(cb.shape),
                _const_spec(dtb.shape), _const_spec(qn.shape), _const_spec(kn.shape),
                pl.BlockSpec((tm, 3 * LANES), lambda b, i: (i, 0)),
                _const_spec(gq.shape), _const_spec(gk.shape), _const_spec(gt.shape)]
    out_specs = (row(SSD_D_INNER), row(SSD_CONV_DIM), row(LANES), row(DA_WIDTH), row(DA_WIDTH),
                 row(DA_WIDTH), head_rows, head_rows)
    return pl.pallas_call(
        _even_in_kernel, grid=grid, in_specs=in_specs, out_specs=out_specs, out_shape=out_shape,
        scratch_shapes=[pltpu.VMEM((tm + SUBLANES, SSD_CONV_DIM), F32)],
        compiler_params=pltpu.CompilerParams(dimension_semantics=("arbitrary", "arbitrary"),
                                             vmem_limit_bytes=VMEM_LIMIT),
        name="even_in",
    )(h, nw, wz, wx, wdt, wq, wk, wv, cw, cb, dtb, qn, kn, rope, gq, gk, gt)


def _ssd_kernel(xc_ref, dt_ref, z_ref, alog_ref, dskip_ref, nw_ref, tril_ref, expand_ref,
                y_ref, state_ref):
    T = SSD_CHUNK
    GN = SSD_GROUPS * SSD_D_STATE
    GW = SSD_D_INNER // SSD_GROUPS
    heads_per_group = SSD_HEADS // SSD_GROUPS

    @pl.when(pl.program_id(1) == 0)
    def _():
        state_ref[...] = jnp.zeros(state_ref.shape, F32)

    a = -jnp.exp(alog_ref[...])
    r_i = lax.broadcasted_iota(jnp.int32, (T, T), 0)
    c_i = lax.broadcasted_iota(jnp.int32, (T, T), 1)
    tril = r_i >= c_i
    lane = lax.broadcasted_iota(jnp.int32, (T, LANES), 1)

    for ci in range(xc_ref.shape[1] // T):
        rows = slice(ci * T, (ci + 1) * T)
        x = xc_ref[0, rows, 0:SSD_D_INNER].astype(F32)
        bm = xc_ref[0, rows, SSD_D_INNER:SSD_D_INNER + GN]
        cm = xc_ref[0, rows, SSD_D_INNER + GN:SSD_D_INNER + 2 * GN]
        dt = dt_ref[0, rows, :]
        a_cs = _dot_sel_lhs(tril_ref[...], dt * a)
        a_cs_t = a_cs.T
        a_last = a_cs[T - 1:T, :]

        stacked = jnp.concatenate(
            [dt, jnp.exp(a_cs), jnp.exp(a_last - a_cs),
             jnp.broadcast_to(jnp.exp(a_last), (SUBLANES, LANES))], axis=0)
        ex = _dot_sel_rhs16(stacked, expand_ref[...])
        dt_e, expa_e, decay_e = ex[0:T], ex[T:2 * T], ex[2 * T:3 * T]
        chunk_decay = ex[3 * T:3 * T + 1]
        xdt = x * dt_e

        xdt_b = xdt.astype(BF16)
        y_parts = []
        for g in range(SSD_GROUPS):
            cg = cm[:, g * SSD_D_STATE:(g + 1) * SSD_D_STATE]
            bg = bm[:, g * SSD_D_STATE:(g + 1) * SSD_D_STATE]
            cb = lax.dot_general(cg, bg, (((1,), (1,)), ((), ())), preferred_element_type=F32)
            for hp in range(heads_per_group // 2):
                h0 = g * heads_per_group + 2 * hp
                ms = []
                for h in (h0, h0 + 1):
                    seg = a_cs[:, h:h + 1] - a_cs_t[h:h + 1, :]
                    ms.append((cb * jnp.exp(jnp.where(tril, seg, -jnp.inf))).astype(BF16))
                lhs = jnp.concatenate(ms, axis=1)
                xp = xdt_b[:, h0 * SSD_HEAD_DIM:(h0 + 2) * SSD_HEAD_DIM]
                rhs = jnp.concatenate([jnp.where(lane < SSD_HEAD_DIM, xp, 0),
                                       jnp.where(lane >= SSD_HEAD_DIM, xp, 0)], axis=0)
                y_parts.append(_dot(lhs, rhs))
        y_diag = jnp.concatenate(y_parts, axis=1)

        xdec = (xdt * decay_e).astype(BF16)
        y_off_parts = []
        for g in range(SSD_GROUPS):
            cg = cm[:, g * SSD_D_STATE:(g + 1) * SSD_D_STATE]
            y_off_parts.append(_dot(cg, state_ref[g].astype(BF16)))
            bg_t = bm[:, g * SSD_D_STATE:(g + 1) * SSD_D_STATE].astype(F32).T.astype(BF16)
            upd = _dot(bg_t, xdec[:, g * GW:(g + 1) * GW])
            state_ref[g] = state_ref[g] * chunk_decay[:, g * GW:(g + 1) * GW] + upd
        y_off = jnp.concatenate(y_off_parts, axis=1) * expa_e

        y = y_diag + y_off + dskip_ref[...] * x
        y = y * _silu(z_ref[0, rows, :].astype(F32))
        outs = []
        for g in range(SSD_GROUPS):
            yg = y[:, g * GW:(g + 1) * GW]
            outs.append(yg * lax.rsqrt(jnp.mean(yg * yg, axis=-1, keepdims=True) + RMS_EPS))
        y_ref[0, rows, :] = (jnp.concatenate(outs, axis=1) * nw_ref[...]).astype(BF16)


def _ssd(xc, dtf, z, alog, dskip, nw, ts):
    B, L, _ = xc.shape
    T = SSD_CHUNK
    grid = (B, L // ts)
    row = lambda w: pl.BlockSpec((1, ts, w), lambda b, c: (b, c, 0))
    tril = (jnp.arange(T)[:, None] >= jnp.arange(T)[None, :]).astype(BF16)
    expand = (jnp.arange(SSD_D_INNER)[None, :] // SSD_HEAD_DIM
              == jnp.arange(2 * LANES)[:, None] % LANES).astype(BF16)
    return pl.pallas_call(
        _ssd_kernel, grid=grid,
        in_specs=[row(SSD_CONV_DIM), row(LANES), row(SSD_D_INNER),
                  _const_spec(alog.shape), _const_spec(dskip.shape), _const_spec(nw.shape),
                  _const_spec(tril.shape), _const_spec(expand.shape)],
        out_specs=row(SSD_D_INNER),
        out_shape=jax.ShapeDtypeStruct((B, L, SSD_D_INNER), BF16),
        scratch_shapes=[pltpu.VMEM((SSD_GROUPS, SSD_D_STATE, SSD_D_INNER // SSD_GROUPS), F32)],
        compiler_params=pltpu.CompilerParams(dimension_semantics=("arbitrary", "arbitrary"),
                                             vmem_limit_bytes=VMEM_LIMIT),
        name="ssd",
    )(xc, dtf, z, alog, dskip, nw, tril, expand)


def _attn_kernel(q_ref, k_ref, v_ref, qsq_ref, ksq_ref, lq1_ref, lk1_ref, lq2_ref, lk2_ref, sub_ref,
                 o_ref, q2_ref, *, tq, lambda_init):
    i = pl.program_id(2)
    tk = tq // 2
    q = q_ref[0]
    lane = lax.broadcasted_iota(jnp.int32, q.shape, 1)
    zero = jnp.zeros_like(q)
    q_map1 = jnp.where(lane < DA_HEAD_DIM, q, zero)
    q_map2 = jnp.where(lane >= DA_HEAD_DIM, q, zero)
    q2_ref[0:tk, :] = q_map1[0:tk]
    q2_ref[tk:2 * tk, :] = q_map2[0:tk]
    q2_ref[2 * tk:3 * tk, :] = q_map1[tk:tq]
    q2_ref[3 * tk:4 * tk, :] = q_map2[tk:tq]
    late = slice(2 * tk, 4 * tk)

    def scores(j, late_only=False):
        kb = k_ref[0, pl.ds(pl.multiple_of(j * tk, tk), tk), :]
        qs = q2_ref[late, :] if late_only else q2_ref[...]
        return lax.dot_general(kb, qs, (((1,), (1,)), ((), ())), preferred_element_type=F32)

    def causal(s):
        kpos = lax.broadcasted_iota(jnp.int32, s.shape, 0)
        col = lax.broadcasted_iota(jnp.int32, s.shape, 1)
        qpos = jnp.where(col >= tk, col - tk, col)
        return jnp.where(kpos <= qpos, s, -jnp.inf)

    def diagonal_scores():
        s_a = scores(2 * i)
        s_a = jnp.concatenate([causal(s_a[:, 0:2 * tk]), s_a[:, late]], axis=1)
        return s_a, causal(scores(2 * i + 1, late_only=True))

    def v_block(j):
        return v_ref[0, pl.ds(pl.multiple_of(j * tk, tk), tk), :]

    def update(j, s, carry):
        m, l, acc = carry
        m_new = jnp.maximum(m, jnp.max(s, axis=0, keepdims=True))
        p = jnp.exp2(s - m_new)
        alpha = jnp.exp2(m - m_new)
        l_new = alpha * l + jnp.sum(p, axis=0, keepdims=True)
        pv = lax.dot_general(v_block(j), p.astype(BF16), (((0,), (0,)), ((), ())),
                             preferred_element_type=F32)
        return m_new, l_new, alpha * acc + pv

    def pair(j0, carry):
        s_a = scores(j0)
        s_b = scores(j0 + 1)
        return update(j0 + 1, s_b, update(j0, s_a, carry))

    def online_path():
        init = (jnp.full((1, 2 * tq), -jnp.inf, F32), jnp.zeros((1, 2 * tq), F32),
                jnp.zeros((DA_V_DIM, 2 * tq), F32))
        carry = lax.fori_loop(0, i, lambda t, c: pair(2 * t, c), init)
        s_a, s_b = diagonal_scores()
        m, l, acc = update(2 * i, s_a, carry)
        _, l_late, acc_late = update(2 * i + 1, s_b, (m[:, late], l[:, late], acc[:, late]))
        return (jnp.concatenate([l[:, 0:2 * tk], l_late], axis=1),
                jnp.concatenate([acc[:, 0:2 * tk], acc_late], axis=1))

    qsq = qsq_ref[0, 0]
    kmax = jnp.max(ksq_ref[0, 0], axis=1, keepdims=True)
    bound = jnp.sqrt(jnp.concatenate(
        [qsq[0:1, 0:tk] * kmax[0:1], qsq[1:2, 0:tk] * kmax[1:2],
         qsq[0:1, tk:tq] * kmax[0:1], qsq[1:2, tk:tq] * kmax[1:2]], axis=1))

    def bounded_update(j, s, shift, carry):
        l, acc = carry
        p = jnp.exp2(s - shift)
        pv = lax.dot_general(v_block(j), p.astype(BF16), (((0,), (0,)), ((), ())),
                             preferred_element_type=F32)
        return l + jnp.sum(p, axis=0, keepdims=True), acc + pv

    def bounded_pair(j0, carry):
        carry = bounded_update(j0, scores(j0), bound, carry)
        return bounded_update(j0 + 1, scores(j0 + 1), bound, carry)

    def bounded_path():
        init = (jnp.zeros((1, 2 * tq), F32), jnp.zeros((DA_V_DIM, 2 * tq), F32))
        carry = lax.fori_loop(0, i, lambda t, c: bounded_pair(2 * t, c), init)
        s_a, s_b = diagonal_scores()
        l, acc = bounded_update(2 * i, s_a, bound, carry)
        l_late, acc_late = bounded_update(2 * i + 1, s_b, bound[:, late], (l[:, late], acc[:, late]))
        return (jnp.concatenate([l[:, 0:2 * tk], l_late], axis=1),
                jnp.concatenate([acc[:, 0:2 * tk], acc_late], axis=1))

    l, acc = lax.cond(jnp.max(bound) <= MAX_SAFE_SCORE_BOUND, bounded_path, online_path)
    o_t = acc / l

    lam = (jnp.exp(jnp.sum(lq1_ref[...] * lk1_ref[...], axis=-1, keepdims=True))
           - jnp.exp(jnp.sum(lq2_ref[...] * lk2_ref[...], axis=-1, keepdims=True)) + lambda_init)
    o_map1 = jnp.concatenate([o_t[:, 0:tk], o_t[:, 2 * tk:3 * tk]], axis=1)
    o_map2 = jnp.concatenate([o_t[:, tk:2 * tk], o_t[:, 3 * tk:4 * tk]], axis=1)
    o = (o_map1 - lam * o_map2).T
    o = _rms(o, sub_ref[...]) * (1.0 - lambda_init)
    o_ref[0] = o.astype(BF16)


def _diff_attn(q, k, v, q_sq, k_sq, lq1, lk1, lq2, lk2, sub, lambda_init, tq):
    B, L, _ = q.shape
    grid = (B, DA_HEADS, L // tq)
    full = pl.BlockSpec((1, L, DA_V_DIM), lambda b, h, i: (b, 0, h))
    blk = pl.BlockSpec((1, tq, DA_V_DIM), lambda b, h, i: (b, i, h))
    return pl.pallas_call(
        functools.partial(_attn_kernel, tq=tq, lambda_init=lambda_init),
        grid=grid,
        scratch_shapes=[pltpu.VMEM((2 * tq, DA_V_DIM), BF16)],
        in_specs=[blk, full, full,
                  pl.BlockSpec((1, 1, 2, tq), lambda b, h, i: (b, h, 0, i)),
                  pl.BlockSpec((1, 1, 2, L), lambda b, h, i: (b, h, 0, 0)),
                  _const_spec(lq1.shape), _const_spec(lk1.shape),
                  _const_spec(lq2.shape), _const_spec(lk2.shape), _const_spec(sub.shape)],
        out_specs=blk,
        out_shape=jax.ShapeDtypeStruct((B, L, DA_WIDTH), BF16),
        compiler_params=pltpu.CompilerParams(
            dimension_semantics=("arbitrary", "arbitrary", "arbitrary"),
            vmem_limit_bytes=VMEM_LIMIT),
        name="diff_attn",
    )(q, k, v, q_sq, k_sq, lq1, lk1, lq2, lk2, sub)


def _ffn_tile(o_ref, i, nw_ref, wup_ref, cw_ref, cb_ref, wdn_ref, buf_ref, tm):
    hn = _rms(o_ref[0], nw_ref[...]).astype(BF16)
    lead = SUBLANES - (FFN_CONV - 1)

    @pl.when(i == 0)
    def _():
        buf_ref[:, 0:SUBLANES, :] = jnp.zeros((buf_ref.shape[0], SUBLANES, buf_ref.shape[2]), F32)

    starts = [sum(FFN_CHUNKS[:c]) for c in range(len(FFN_CHUNKS))]

    def slot(c, part):
        return buf_ref.at[2 * c + part, :, 0:FFN_CHUNKS[c]]

    def up_project(c):
        for part in range(2):
            col = part * D_FF + starts[c]
            slot(c, part)[SUBLANES:SUBLANES + tm, :] = _dot(hn, wup_ref[:, col:col + FFN_CHUNKS[c]])

    up_project(0)
    for c, width in enumerate(FFN_CHUNKS):
        if c + 1 < len(FFN_CHUNKS):
            up_project(c + 1)
        acts = []
        for part in range(2):
            col = part * D_FF + starts[c]
            taps = [cw_ref[k:k + 1, col:col + width] for k in range(FFN_CONV)]
            acts.append(_causal_conv(slot(c, part), tm, taps, lead) + cb_ref[:, col:col + width])
            slot(c, part)[0:SUBLANES, :] = slot(c, part)[tm:tm + SUBLANES, :]
        act = (_silu(acts[0]) * acts[1]).astype(BF16)
        o_ref[0] += _dot(act, wdn_ref[starts[c]:starts[c] + width, :])


def _out_ffn_kernel(h_ref, ys_ref, ya_ref, wo_ref, nw_ref, wup_ref, cw_ref, cb_ref, wdn_ref,
                    o_ref, buf_ref):
    i = pl.program_id(1)
    tm = h_ref.shape[1]
    o_ref[0] = (h_ref[0] + _dot(ys_ref[0], wo_ref[0:SSD_D_INNER, :])
                + _dot(ya_ref[0], wo_ref[SSD_D_INNER:SSD_D_INNER + DA_WIDTH, :]))
    _ffn_tile(o_ref, i, nw_ref, wup_ref, cw_ref, cb_ref, wdn_ref, buf_ref, tm)


def _out_ffn(h, ys, ya, wo, nw, wup, cw, cb, wdn, tm):
    B, L, D = h.shape
    grid = (B, L // tm)
    row = lambda w: pl.BlockSpec((1, tm, w), lambda b, i: (b, i, 0))
    return pl.pallas_call(
        _out_ffn_kernel, grid=grid,
        in_specs=[row(D), row(SSD_D_INNER), row(DA_WIDTH), _const_spec(wo.shape),
                  _const_spec(nw.shape), _const_spec(wup.shape), _const_spec(cw.shape),
                  _const_spec(cb.shape), _const_spec(wdn.shape)],
        out_specs=row(D),
        out_shape=jax.ShapeDtypeStruct((B, L, D), F32),
        scratch_shapes=[pltpu.VMEM((2 * len(FFN_CHUNKS), tm + SUBLANES, max(FFN_CHUNKS)), F32)],
        compiler_params=pltpu.CompilerParams(dimension_semantics=("arbitrary", "arbitrary"),
                                             vmem_limit_bytes=VMEM_LIMIT),
        name="out_ffn",
    )(h, ys, ya, wo, nw, wup, cw, cb, wdn)


def _sc_ffn_kernel(h_ref, mnw_ref, win_ref, scw_ref, wout_ref, nw_ref, wup_ref, cw_ref, cb_ref,
                   wdn_ref, o_ref, mbuf_ref, buf_ref):
    i = pl.program_id(1)
    tm = h_ref.shape[1]
    h = h_ref[0]
    hn = _rms(h, mnw_ref[...]).astype(BF16)

    @pl.when(i == 0)
    def _():
        mbuf_ref[0:SUBLANES, :] = jnp.zeros((SUBLANES, SC_WIDTH), F32)

    cg = _dot(hn, win_ref[:, SC_WIDTH:2 * SC_WIDTH])
    u = _dot(hn, win_ref[:, 2 * SC_WIDTH:3 * SC_WIDTH])
    bg = _dot(hn, win_ref[:, 0:SC_WIDTH])
    mbuf_ref[SUBLANES:SUBLANES + tm, :] = cg * u
    taps = [scw_ref[k:k + 1, :] for k in range(SC_CONV)]
    conv = _causal_conv(mbuf_ref, tm, taps, SUBLANES - (SC_CONV - 1))
    mbuf_ref[0:SUBLANES, :] = mbuf_ref[tm:tm + SUBLANES, :]
    o_ref[0] = h + _dot((bg * conv).astype(BF16), wout_ref[...])
    _ffn_tile(o_ref, i, nw_ref, wup_ref, cw_ref, cb_ref, wdn_ref, buf_ref, tm)


def _sc_ffn(h, mnw, win, scw, wout, nw, wup, cw, cb, wdn, tm):
    B, L, D = h.shape
    grid = (B, L // tm)
    row = pl.BlockSpec((1, tm, D), lambda b, i: (b, i, 0))
    return pl.pallas_call(
        _sc_ffn_kernel, grid=grid,
        in_specs=[row, _const_spec(mnw.shape), _const_spec(win.shape), _const_spec(scw.shape),
                  _const_spec(wout.shape), _const_spec(nw.shape), _const_spec(wup.shape),
                  _const_spec(cw.shape), _const_spec(cb.shape), _const_spec(wdn.shape)],
        out_specs=row,
        out_shape=jax.ShapeDtypeStruct((B, L, D), F32),
        scratch_shapes=[pltpu.VMEM((tm + SUBLANES, SC_WIDTH), F32),
                        pltpu.VMEM((2 * len(FFN_CHUNKS), tm + SUBLANES, max(FFN_CHUNKS)), F32)],
        compiler_params=pltpu.CompilerParams(dimension_semantics=("arbitrary", "arbitrary"),
                                             vmem_limit_bytes=VMEM_LIMIT),
        name="sc_ffn",
    )(h, mnw, win, scw, wout, nw, wup, cw, cb, wdn)


def _rope_table(seq):
    inv = 1.0 / (ROPE_THETA ** (jnp.arange(0, DA_HEAD_DIM, 2, dtype=F32) / DA_HEAD_DIM))
    ang = jnp.arange(seq, dtype=F32)[:, None] * inv[None, :]
    ang = jnp.concatenate([ang, ang], axis=-1)
    cos, sin = jnp.cos(ang), jnp.sin(ang)
    first = jnp.arange(DA_HEAD_DIM) < DA_HEAD_DIM // 2
    sin_a = jnp.where(first, -sin, 0.0)
    sin_b = jnp.where(first, 0.0, sin)
    rep = LANES // DA_HEAD_DIM
    return jnp.concatenate([jnp.tile(cos, (1, rep)), jnp.tile(sin_a, (1, rep)),
                            jnp.tile(sin_b, (1, rep))], axis=-1)


def _row(v):
    return v.reshape(1, -1).astype(F32)


def _pad_lanes(v):
    return jnp.pad(v.reshape(1, -1).astype(F32), ((0, 0), (0, LANES - v.shape[-1])))


def kernel(x, mix_norm, ffn_norm, hy_w_in, hy_conv_w, hy_conv_b, hy_dt_bias, hy_a_log, hy_d_skip,
           hy_ssd_norm, hy_q_norm, hy_k_norm, hy_lambda_q1, hy_lambda_k1, hy_lambda_q2,
           hy_lambda_k2, hy_subln, hy_w_out, sc_w_in, sc_conv_w, sc_w_out, ffn_w_up, ffn_conv_w,
           ffn_conv_b, ffn_w_down):
    B, L, D = x.shape
    tm = min(512, L)
    tq = min(1024, L)

    w = hy_w_in[0]
    o_x = SSD_D_INNER
    o_dt = o_x + SSD_CONV_DIM
    o_q = o_dt + SSD_HEADS
    o_k = o_q + DA_WIDTH
    o_v = o_k + DA_WIDTH
    wz = w[:, 0:o_x].astype(BF16)
    wx = w[:, o_x:o_dt].astype(BF16)
    wdt = jnp.pad(w[:, o_dt:o_q], ((0, 0), (0, LANES - SSD_HEADS))).astype(BF16)
    wq = w[:, o_q:o_k].astype(BF16)
    wk = w[:, o_k:o_v].astype(BF16)
    wv = w[:, o_v:o_v + DA_WIDTH].astype(BF16)
    head_of = jnp.arange(DA_WIDTH) // DA_HEAD_DIM
    g = (head_of[:, None] == jnp.arange(LANES)[None, :]).astype(F32)
    gt = jnp.concatenate([g.T, g.T], axis=0).astype(BF16)
    q_gain = jnp.tile(hy_q_norm[0], DA_WIDTH // DA_HEAD_DIM).astype(F32)
    k_gain = jnp.tile(hy_k_norm[0], DA_WIDTH // DA_HEAD_DIM).astype(F32)

    def head_sum_selector(gain):
        weighted = jnp.roll(g * (gain * gain)[:, None], HEADS_QK, axis=1)
        return (g + weighted).astype(BF16)

    rope = _rope_table(L)
    z, xc, dtf, q, k, v, q_sq, k_sq = _even_in(
        x, _row(mix_norm[0]), wz, wx, wdt, wq, wk, wv, hy_conv_w[0].astype(F32),
        _row(hy_conv_b[0]), _pad_lanes(hy_dt_bias[0]), _row(q_gain), _row(k_gain), rope,
        head_sum_selector(q_gain), head_sum_selector(k_gain), gt, tm)
    y_ssd = _ssd(xc, dtf, z, _pad_lanes(hy_a_log[0]),
                 _row(jnp.repeat(hy_d_skip[0], SSD_HEAD_DIM)), _row(hy_ssd_norm[0]), tm)
    lambda_init = 0.8 - 0.6 * math.exp(-0.3 * 0)
    y_att = _diff_attn(q, k, v, q_sq.reshape(B, DA_HEADS, 2, L), k_sq.reshape(B, DA_HEADS, 2, L),
                       _row(hy_lambda_q1[0]), _row(hy_lambda_k1[0]),
                       _row(hy_lambda_q2[0]), _row(hy_lambda_k2[0]), _row(hy_subln[0]),
                       lambda_init, tq)
    h = _out_ffn(x, y_ssd, y_att, hy_w_out[0].astype(BF16), _row(ffn_norm[0]),
                 ffn_w_up[0].astype(BF16), ffn_conv_w[0].astype(F32), _row(ffn_conv_b[0]),
                 ffn_w_down[0].astype(BF16), tm)

    h = _sc_ffn(h, _row(mix_norm[1]), sc_w_in[0].astype(BF16), sc_conv_w[0].astype(F32),
                sc_w_out[0].astype(BF16), _row(ffn_norm[1]), ffn_w_up[1].astype(BF16),
                ffn_conv_w[1].astype(F32), _row(ffn_conv_b[1]), ffn_w_down[1].astype(BF16), tm)
    return h
```

```python
import functools
import math

import jax
import jax.numpy as jnp
from jax import lax
from jax.experimental import pallas as pl
from jax.experimental.pallas import tpu as pltpu

F32 = jnp.float32
BF16 = jnp.bfloat16

RMS_EPS = 1e-6
ROPE_THETA = 10000.0
D_MODEL = 1024
SSD_HEAD_DIM = 64
SSD_HEADS = 16
SSD_GROUPS = 2
SSD_D_STATE = 128
SSD_D_INNER = 1024
SSD_CONV = 4
SSD_CHUNK = 128
SSD_CONV_DIM = SSD_D_INNER + 2 * SSD_GROUPS * SSD_D_STATE
DA_HEADS = 8
DA_HEAD_DIM = 64
HEADS_QK = 2 * DA_HEADS
DA_V_DIM = 128
DA_WIDTH = 1024
SC_WIDTH = 1024
SC_CONV = 3
D_FF = 2816
FFN_CONV = 3
FFN_CHUNKS = (768, 768, 768, 512)
LANES = 128
SUBLANES = 8
VMEM_LIMIT = 56 * 1024 * 1024
LOG2E = 1.4426950408889634
ATTN_KEY_BLOCK = 512
MAX_SAFE_SCORE_BOUND = 60.0


def _dot(a, b):
    return jnp.dot(a, b, preferred_element_type=F32)


def _split3(x):
    hi = x.astype(BF16)
    r1 = x - hi.astype(F32)
    mid = r1.astype(BF16)
    lo = (r1 - mid.astype(F32)).astype(BF16)
    return hi, mid, lo


def _dot_sel_rhs(x, sel):
    hi, mid, lo = _split3(x)
    return _dot(hi, sel) + _dot(mid, sel) + _dot(lo, sel)


def _dot_sel_rhs16(x, sel2):
    hi = x.astype(BF16)
    mid = (x - hi.astype(F32)).astype(BF16)
    return _dot(jnp.concatenate([hi, mid], axis=1), sel2)


def _dot_sel_lhs(sel, x):
    hi, mid, lo = _split3(x)
    return _dot(sel, hi) + _dot(sel, mid) + _dot(sel, lo)


def _rms(x, w):
    return x * lax.rsqrt(jnp.mean(x * x, axis=-1, keepdims=True) + RMS_EPS) * w


def _silu(x):
    return x * (1.0 / (1.0 + jnp.exp(-x)))


def _softplus(x):
    return jnp.maximum(x, 0.0) + jnp.log1p(jnp.exp(-jnp.abs(x)))


def _const_spec(shape):
    nd = len(shape)
    return pl.BlockSpec(shape, lambda *_: (0,) * nd, pipeline_mode=pl.Buffered(1))


def _causal_conv(buf_ref, tm, taps, first_row):
    xe = buf_ref[...]
    acc = None
    for k, w in enumerate(taps):
        back = SUBLANES - (first_row + k)
        shifted = xe if back == 0 else pltpu.roll(xe, back, 0)
        term = shifted[SUBLANES:SUBLANES + tm] * w
        acc = term if acc is None else acc + term
    return acc


def _even_in_kernel(h_ref, nw_ref, wz_ref, wx_ref, wdt_ref, wq_ref, wk_ref, wv_ref,
                    cw_ref, cb_ref, dtb_ref, qn_ref, kn_ref, rope_ref, gq_ref, gk_ref, gt_ref,
                    z_ref, xc_ref, dt_ref, q_ref, k_ref, v_ref, qsq_ref, ksq_ref, xs_ref):
    i = pl.program_id(1)
    tm = h_ref.shape[1]
    hn = _rms(h_ref[0], nw_ref[...]).astype(BF16)

    @pl.when(i == 0)
    def _():
        xs_ref[0:SUBLANES, :] = jnp.zeros((SUBLANES, SSD_CONV_DIM), F32)

    xs_ref[SUBLANES:SUBLANES + tm, :] = _dot(hn, wx_ref[...])
    q_raw = _dot(hn, wq_ref[...])

    taps = [cw_ref[k:k + 1, :] for k in range(SSD_CONV)]
    conv = _causal_conv(xs_ref, tm, taps, SUBLANES - (SSD_CONV - 1)) + cb_ref[...]
    xc_ref[0] = _silu(conv).astype(BF16)
    xs_ref[0:SUBLANES, :] = xs_ref[tm:tm + SUBLANES, :]

    k_raw = _dot(hn, wk_ref[...])

    cos = jnp.tile(rope_ref[:, 0:LANES], (1, DA_WIDTH // LANES))
    sin_a = jnp.tile(rope_ref[:, LANES:2 * LANES], (1, DA_WIDTH // LANES))
    sin_b = jnp.tile(rope_ref[:, 2 * LANES:3 * LANES], (1, DA_WIDTH // LANES))
    half = DA_HEAD_DIM // 2

    def qk_epilogue(x, n_ref, g_ref, o_ref, sq_ref, scale):
        ss = _dot((x * x).astype(BF16), g_ref[...])
        r = lax.rsqrt(ss * (1.0 / DA_HEAD_DIM) + RMS_EPS) * scale
        xn = x * _dot_sel_rhs16(r, gt_ref[...]) * n_ref[...]
        rot = (pltpu.roll(xn, DA_WIDTH - half, 1) * sin_a + pltpu.roll(xn, half, 1) * sin_b)
        o_ref[0] = (xn * cos + rot).astype(BF16)
        nsq = r * r * pltpu.roll(ss, LANES - HEADS_QK, 1)
        sq_ref[0] = nsq.T[0:HEADS_QK, :]

    qk_epilogue(q_raw, qn_ref, gq_ref, q_ref, qsq_ref, LOG2E * DA_HEAD_DIM ** -0.5)
    z_ref[0] = _dot(hn, wz_ref[...]).astype(BF16)
    qk_epilogue(k_raw, kn_ref, gk_ref, k_ref, ksq_ref, 1.0)
    v_ref[0] = _dot(hn, wv_ref[...]).astype(BF16)
    dt_ref[0] = _softplus(_dot(hn, wdt_ref[...]) + dtb_ref[...])


def _even_in(h, nw, wz, wx, wdt, wq, wk, wv, cw, cb, dtb, qn, kn, rope, gq, gk, gt, tm):
    B, L, D = h.shape
    grid = (B, L // tm)
    row = lambda w: pl.BlockSpec((1, tm, w), lambda b, i: (b, i, 0))
    out_shape = (jax.ShapeDtypeStruct((B, L, SSD_D_INNER), BF16),
                 jax.ShapeDtypeStruct((B, L, SSD_CONV_DIM), BF16),
                 jax.ShapeDtypeStruct((B, L, LANES), F32),
                 jax.ShapeDtypeStruct((B, L, DA_WIDTH), BF16),
                 jax.ShapeDtypeStruct((B, L, DA_WIDTH), BF16),
                 jax.ShapeDtypeStruct((B, L, DA_WIDTH), BF16),
                 jax.ShapeDtypeStruct((B, HEADS_QK, L), F32),
                 jax.ShapeDtypeStruct((B, HEADS_QK, L), F32))
    head_rows = pl.BlockSpec((1, HEADS_QK, tm), lambda b, i: (b, 0, i))
    in_specs = [row(D), _const_spec(nw.shape), _const_spec(wz.shape), _const_spec(wx.shape),
                _const_spec(wdt.shape), _const_spec(wq.shape), _const_spec(wk.shape),
                _const_spec(wv.shape), _const_spec(cw.shape), _const_spec(cb.shape),
                _const_spec(dtb.shape), _const_spec(qn.shape), _const_spec(kn.shape),
                pl.BlockSpec((tm, 3 * LANES), lambda b, i: (i, 0)),
                _const_spec(gq.shape), _const_spec(gk.shape), _const_spec(gt.shape)]
    out_specs = (row(SSD_D_INNER), row(SSD_CONV_DIM), row(LANES), row(DA_WIDTH), row(DA_WIDTH),
                 row(DA_WIDTH), head_rows, head_rows)
    return pl.pallas_call(
        _even_in_kernel, grid=grid, in_specs=in_specs, out_specs=out_specs, out_shape=out_shape,
        scratch_shapes=[pltpu.VMEM((tm + SUBLANES, SSD_CONV_DIM), F32)],
        compiler_params=pltpu.CompilerParams(dimension_semantics=("arbitrary", "arbitrary"),
                                             vmem_limit_bytes=VMEM_LIMIT),
        name="even_in",
    )(h, nw, wz, wx, wdt, wq, wk, wv, cw, cb, dtb, qn, kn, rope, gq, gk, gt)


def _ssd_kernel(xc_ref, dt_ref, z_ref, alog_ref, dskip_ref, nw_ref, tril_ref, expand_ref,
                y_ref, state_ref):
    T = SSD_CHUNK
    GN = SSD_GROUPS * SSD_D_STATE
    GW = SSD_D_INNER // SSD_GROUPS
    heads_per_group = SSD_HEADS // SSD_GROUPS

    @pl.when(pl.program_id(1) == 0)
    def _():
        state_ref[...] = jnp.zeros(state_ref.shape, F32)

    a = -jnp.exp(alog_ref[...])
    r_i = lax.broadcasted_iota(jnp.int32, (T, T), 0)
    c_i = lax.broadcasted_iota(jnp.int32, (T, T), 1)
    tril = r_i >= c_i
    lane = lax.broadcasted_iota(jnp.int32, (T, LANES), 1)

    for ci in range(xc_ref.shape[1] // T):
        rows = slice(ci * T, (ci + 1) * T)
        x = xc_ref[0, rows, 0:SSD_D_INNER].astype(F32)
        bm = xc_ref[0, rows, SSD_D_INNER:SSD_D_INNER + GN]
        cm = xc_ref[0, rows, SSD_D_INNER + GN:SSD_D_INNER + 2 * GN]
        dt = dt_ref[0, rows, :]
        a_cs = _dot_sel_lhs(tril_ref[...], dt * a)
        a_cs_t = a_cs.T
        a_last = a_cs[T - 1:T, :]

        stacked = jnp.concatenate(
            [dt, jnp.exp(a_cs), jnp.exp(a_last - a_cs),
             jnp.broadcast_to(jnp.exp(a_last), (SUBLANES, LANES))], axis=0)
        ex = _dot_sel_rhs16(stacked, expand_ref[...])
        dt_e, expa_e, decay_e = ex[0:T], ex[T:2 * T], ex[2 * T:3 * T]
        chunk_decay = ex[3 * T:3 * T + 1]
        xdt = x * dt_e

        xdt_b = xdt.astype(BF16)
        y_parts = []
        for g in range(SSD_GROUPS):
            cg = cm[:, g * SSD_D_STATE:(g + 1) * SSD_D_STATE]
            bg = bm[:, g * SSD_D_STATE:(g + 1) * SSD_D_STATE]
            cb = lax.dot_general(cg, bg, (((1,), (1,)), ((), ())), preferred_element_type=F32)
            for hp in range(heads_per_group // 2):
                h0 = g * heads_per_group + 2 * hp
                ms = []
                for h in (h0, h0 + 1):
                    seg = a_cs[:, h:h + 1] - a_cs_t[h:h + 1, :]
                    ms.append((cb * jnp.exp(jnp.where(tril, seg, -jnp.inf))).astype(BF16))
                lhs = jnp.concatenate(ms, axis=1)
                xp = xdt_b[:, h0 * SSD_HEAD_DIM:(h0 + 2) * SSD_HEAD_DIM]
                rhs = jnp.concatenate([jnp.where(lane < SSD_HEAD_DIM, xp, 0),
                                       jnp.where(lane >= SSD_HEAD_DIM, xp, 0)], axis=0)
                y_parts.append(_dot(lhs, rhs))
        y_diag = jnp.concatenate(y_parts, axis=1)

        xdec = (xdt * decay_e).astype(BF16)
        y_off_parts = []
        for g in range(SSD_GROUPS):
            cg = cm[:, g * SSD_D_STATE:(g + 1) * SSD_D_STATE]
            y_off_parts.append(_dot(cg, state_ref[g].astype(BF16)))
            bg_t = bm[:, g * SSD_D_STATE:(g + 1) * SSD_D_STATE].astype(F32).T.astype(BF16)
            upd = _dot(bg_t, xdec[:, g * GW:(g + 1) * GW])
            state_ref[g] = state_ref[g] * chunk_decay[:, g * GW:(g + 1) * GW] + upd
        y_off = jnp.concatenate(y_off_parts, axis=1) * expa_e

        y = y_diag + y_off + dskip_ref[...] * x
        y = y * _silu(z_ref[0, rows, :].astype(F32))
        outs = []
        for g in range(SSD_GROUPS):
            yg = y[:, g * GW:(g + 1) * GW]
            outs.append(yg * lax.rsqrt(jnp.mean(yg * yg, axis=-1, keepdims=True) + RMS_EPS))
        y_ref[0, rows, :] = (jnp.concatenate(outs, axis=1) * nw_ref[...]).astype(BF16)


def _ssd(xc, dtf, z, alog, dskip, nw, ts):
    B, L, _ = xc.shape
    T = SSD_CHUNK
    grid = (B, L // ts)
    row = lambda w: pl.BlockSpec((1, ts, w), lambda b, c: (b, c, 0))
    tril = (jnp.arange(T)[:, None] >= jnp.arange(T)[None, :]).astype(BF16)
    expand = (jnp.arange(SSD_D_INNER)[None, :] // SSD_HEAD_DIM
              == jnp.arange(2 * LANES)[:, None] % LANES).astype(BF16)
    return pl.pallas_call(
        _ssd_kernel, grid=grid,
        in_specs=[row(SSD_CONV_DIM), row(LANES), row(SSD_D_INNER),
                  _const_spec(alog.shape), _const_spec(dskip.shape), _const_spec(nw.shape),
                  _const_spec(tril.shape), _const_spec(expand.shape)],
        out_specs=row(SSD_D_INNER),
        out_shape=jax.ShapeDtypeStruct((B, L, SSD_D_INNER), BF16),
        scratch_shapes=[pltpu.VMEM((SSD_GROUPS, SSD_D_STATE, SSD_D_INNER // SSD_GROUPS), F32)],
        compiler_params=pltpu.CompilerParams(dimension_semantics=("arbitrary", "arbitrary"),
                                             vmem_limit_bytes=VMEM_LIMIT),
        name="ssd",
    )(xc, dtf, z, alog, dskip, nw, tril, expand)


def _attn_kernel(q_ref, k_ref, v_ref, qsq_ref, ksq_ref, lq1_ref, lk1_ref, lq2_ref, lk2_ref, sub_ref,
                 o_ref, q2_ref, *, tq, lambda_init):
    i = pl.program_id(2)
    tk = ATTN_KEY_BLOCK
    nq = tq // tk
    q = q_ref[0]
    lane = lax.broadcasted_iota(jnp.int32, q.shape, 1)
    zero = jnp.zeros_like(q)
    q_map1 = jnp.where(lane < DA_HEAD_DIM, q, zero)
    q_map2 = jnp.where(lane >= DA_HEAD_DIM, q, zero)
    for g in range(nq):
        q2_ref[2 * g * tk:(2 * g + 1) * tk, :] = q_map1[g * tk:(g + 1) * tk]
        q2_ref[(2 * g + 1) * tk:(2 * g + 2) * tk, :] = q_map2[g * tk:(g + 1) * tk]

    def from_group(g):
        return slice(2 * g * tk, 2 * tq)

    def scores(j, first_group=0):
        kb = k_ref[0, pl.ds(pl.multiple_of(j * tk, tk), tk), :]
        return lax.dot_general(kb, q2_ref[from_group(first_group), :], (((1,), (1,)), ((), ())),
                               preferred_element_type=F32)

    def causal(s):
        kpos = lax.broadcasted_iota(jnp.int32, s.shape, 0)
        col = lax.broadcasted_iota(jnp.int32, s.shape, 1)
        qpos = jnp.where(col >= tk, col - tk, col)
        return jnp.where(kpos <= qpos, s, -jnp.inf)

    def diagonal_scores(g):
        s = scores(nq * i + g, first_group=g)
        if g + 1 == nq:
            return causal(s)
        return jnp.concatenate([causal(s[:, 0:2 * tk]), s[:, 2 * tk:]], axis=1)

    def merge(full, part, g):
        return part if g == 0 else jnp.concatenate([full[:, 0:2 * g * tk], part], axis=1)

    def v_block(j):
        return v_ref[0, pl.ds(pl.multiple_of(j * tk, tk), tk), :]

    def update(j, s, carry):
        m, l, acc = carry
        m_new = jnp.maximum(m, jnp.max(s, axis=0, keepdims=True))
        p = jnp.exp2(s - m_new)
        alpha = jnp.exp2(m - m_new)
        l_new = alpha * l + jnp.sum(p, axis=0, keepdims=True)
        pv = lax.dot_general(v_block(j), p.astype(BF16), (((0,), (0,)), ((), ())),
                             preferred_element_type=F32)
        return m_new, l_new, alpha * acc + pv

    def pair(j0, carry):
        s_a = scores(j0)
        s_b = scores(j0 + 1)
        return update(j0 + 1, s_b, update(j0, s_a, carry))

    def online_path():
        init = (jnp.full((1, 2 * tq), -jnp.inf, F32), jnp.zeros((1, 2 * tq), F32),
                jnp.zeros((DA_V_DIM, 2 * tq), F32))
        m, l, acc = lax.fori_loop(0, (nq // 2) * i, lambda t, c: pair(2 * t, c), init)
        for g in range(nq):
            cols = from_group(g)
            part = update(nq * i + g, diagonal_scores(g), (m[:, cols], l[:, cols], acc[:, cols]))
            m, l, acc = merge(m, part[0], g), merge(l, part[1], g), merge(acc, part[2], g)
        return l, acc

    qsq = qsq_ref[0, 0]
    kmax = jnp.max(ksq_ref[0, 0], axis=1, keepdims=True)
    bound = jnp.sqrt(jnp.concatenate(
        [qsq[mp:mp + 1, g * tk:(g + 1) * tk] * kmax[mp:mp + 1]
         for g in range(nq) for mp in range(2)], axis=1))

    def bounded_update(j, s, shift, carry):
        l, acc = carry
        p = jnp.exp2(s - shift)
        pv = lax.dot_general(v_block(j), p.astype(BF16), (((0,), (0,)), ((), ())),
                             preferred_element_type=F32)
        return l + jnp.sum(p, axis=0, keepdims=True), acc + pv

    def bounded_pair(j0, carry):
        carry = bounded_update(j0, scores(j0), bound, carry)
        return bounded_update(j0 + 1, scores(j0 + 1), bound, carry)

    def bounded_path():
        init = (jnp.zeros((1, 2 * tq), F32), jnp.zeros((DA_V_DIM, 2 * tq), F32))
        l, acc = lax.fori_loop(0, (nq // 2) * i, lambda t, c: bounded_pair(2 * t, c), init)
        for g in range(nq):
            cols = from_group(g)
            part = bounded_update(nq * i + g, diagonal_scores(g), bound[:, cols],
                                  (l[:, cols], acc[:, cols]))
            l, acc = merge(l, part[0], g), merge(acc, part[1], g)
        return l, acc

    l, acc = lax.cond(jnp.max(bound) <= MAX_SAFE_SCORE_BOUND, bounded_path, online_path)
    o_t = acc / l

    lam = (jnp.exp(jnp.sum(lq1_ref[...] * lk1_ref[...], axis=-1, keepdims=True))
           - jnp.exp(jnp.sum(lq2_ref[...] * lk2_ref[...], axis=-1, keepdims=True)) + lambda_init)
    o_map1 = jnp.concatenate([o_t[:, 2 * g * tk:(2 * g + 1) * tk] for g in range(nq)], axis=1)
    o_map2 = jnp.concatenate([o_t[:, (2 * g + 1) * tk:(2 * g + 2) * tk] for g in range(nq)], axis=1)
    o = (o_map1 - lam * o_map2).T
    o = _rms(o, sub_ref[...]) * (1.0 - lambda_init)
    o_ref[0] = o.astype(BF16)


def _diff_attn(q, k, v, q_sq, k_sq, lq1, lk1, lq2, lk2, sub, lambda_init, tq):
    B, L, _ = q.shape
    grid = (B, DA_HEADS, L // tq)
    full = pl.BlockSpec((1, L, DA_V_DIM), lambda b, h, i: (b, 0, h))
    blk = pl.BlockSpec((1, tq, DA_V_DIM), lambda b, h, i: (b, i, h))
    return pl.pallas_call(
        functools.partial(_attn_kernel, tq=tq, lambda_init=lambda_init),
        grid=grid,
        scratch_shapes=[pltpu.VMEM((2 * tq, DA_V_DIM), BF16)],
        in_specs=[blk, full, full,
                  pl.BlockSpec((1, 1, 2, tq), lambda b, h, i: (b, h, 0, i)),
                  pl.BlockSpec((1, 1, 2, L), lambda b, h, i: (b, h, 0, 0)),
                  _const_spec(lq1.shape), _const_spec(lk1.shape),
                  _const_spec(lq2.shape), _const_spec(lk2.shape), _const_spec(sub.shape)],
        out_specs=blk,
        out_shape=jax.ShapeDtypeStruct((B, L, DA_WIDTH), BF16),
        compiler_params=pltpu.CompilerParams(
            dimension_semantics=("arbitrary", "arbitrary", "arbitrary"),
            vmem_limit_bytes=VMEM_LIMIT),
        name="diff_attn",
    )(q, k, v, q_sq, k_sq, lq1, lk1, lq2, lk2, sub)


def _ffn_tile(o_ref, i, nw_ref, wup_ref, cw_ref, cb_ref, wdn_ref, buf_ref, tm):
    hn = _rms(o_ref[0], nw_ref[...]).astype(BF16)
    lead = SUBLANES - (FFN_CONV - 1)

    @pl.when(i == 0)
    def _():
        buf_ref[:, 0:SUBLANES, :] = jnp.zeros((buf_ref.shape[0], SUBLANES, buf_ref.shape[2]), F32)

    starts = [sum(FFN_CHUNKS[:c]) for c in range(len(FFN_CHUNKS))]

    def slot(c, part):
        return buf_ref.at[2 * c + part, :, 0:FFN_CHUNKS[c]]

    def up_project(c):
        for part in range(2):
            col = part * D_FF + starts[c]
            slot(c, part)[SUBLANES:SUBLANES + tm, :] = _dot(hn, wup_ref[:, col:col + FFN_CHUNKS[c]])

    up_project(0)
    for c, width in enumerate(FFN_CHUNKS):
        if c + 1 < len(FFN_CHUNKS):
            up_project(c + 1)
        acts = []
        for part in range(2):
            col = part * D_FF + starts[c]
            taps = [cw_ref[k:k + 1, col:col + width] for k in range(FFN_CONV)]
            acts.append(_causal_conv(slot(c, part), tm, taps, lead) + cb_ref[:, col:col + width])
            slot(c, part)[0:SUBLANES, :] = slot(c, part)[tm:tm + SUBLANES, :]
        act = (_silu(acts[0]) * acts[1]).astype(BF16)
        o_ref[0] += _dot(act, wdn_ref[starts[c]:starts[c] + width, :])


def _out_ffn_kernel(h_ref, ys_ref, ya_ref, wo_ref, nw_ref, wup_ref, cw_ref, cb_ref, wdn_ref,
                    o_ref, buf_ref):
    i = pl.program_id(1)
    tm = h_ref.shape[1]
    o_ref[0] = (h_ref[0] + _dot(ys_ref[0], wo_ref[0:SSD_D_INNER, :])
                + _dot(ya_ref[0], wo_ref[SSD_D_INNER:SSD_D_INNER + DA_WIDTH, :]))
    _ffn_tile(o_ref, i, nw_ref, wup_ref, cw_ref, cb_ref, wdn_ref, buf_ref, tm)


def _out_ffn(h, ys, ya, wo, nw, wup, cw, cb, wdn, tm):
    B, L, D = h.shape
    grid = (B, L // tm)
    row = lambda w: pl.BlockSpec((1, tm, w), lambda b, i: (b, i, 0))
    return pl.pallas_call(
        _out_ffn_kernel, grid=grid,
        in_specs=[row(D), row(SSD_D_INNER), row(DA_WIDTH), _const_spec(wo.shape),
                  _const_spec(nw.shape), _const_spec(wup.shape), _const_spec(cw.shape),
                  _const_spec(cb.shape), _const_spec(wdn.shape)],
        out_specs=row(D),
        out_shape=jax.ShapeDtypeStruct((B, L, D), F32),
        scratch_shapes=[pltpu.VMEM((2 * len(FFN_CHUNKS), tm + SUBLANES, max(FFN_CHUNKS)), F32)],
        compiler_params=pltpu.CompilerParams(dimension_semantics=("arbitrary", "arbitrary"),
                                             vmem_limit_bytes=VMEM_LIMIT),
        name="out_ffn",
    )(h, ys, ya, wo, nw, wup, cw, cb, wdn)


def _sc_ffn_kernel(h_ref, mnw_ref, win_ref, scw_ref, wout_ref, nw_ref, wup_ref, cw_ref, cb_ref,
                   wdn_ref, o_ref, mbuf_ref, buf_ref):
    i = pl.program_id(1)
    tm = h_ref.shape[1]
    h = h_ref[0]
    hn = _rms(h, mnw_ref[...]).astype(BF16)

    @pl.when(i == 0)
    def _():
        mbuf_ref[0:SUBLANES, :] = jnp.zeros((SUBLANES, SC_WIDTH), F32)

    cg = _dot(hn, win_ref[:, SC_WIDTH:2 * SC_WIDTH])
    u = _dot(hn, win_ref[:, 2 * SC_WIDTH:3 * SC_WIDTH])
    bg = _dot(hn, win_ref[:, 0:SC_WIDTH])
    mbuf_ref[SUBLANES:SUBLANES + tm, :] = cg * u
    taps = [scw_ref[k:k + 1, :] for k in range(SC_CONV)]
    conv = _causal_conv(mbuf_ref, tm, taps, SUBLANES - (SC_CONV - 1))
    mbuf_ref[0:SUBLANES, :] = mbuf_ref[tm:tm + SUBLANES, :]
    o_ref[0] = h + _dot((bg * conv).astype(BF16), wout_ref[...])
    _ffn_tile(o_ref, i, nw_ref, wup_ref, cw_ref, cb_ref, wdn_ref, buf_ref, tm)


def _sc_ffn(h, mnw, win, scw, wout, nw, wup, cw, cb, wdn, tm):
    B, L, D = h.shape
    grid = (B, L // tm)
    row = pl.BlockSpec((1, tm, D), lambda b, i: (b, i, 0))
    return pl.pallas_call(
        _sc_ffn_kernel, grid=grid,
        in_specs=[row, _const_spec(mnw.shape), _const_spec(win.shape), _const_spec(scw.shape),
                  _const_spec(wout.shape), _const_spec(nw.shape), _const_spec(wup.shape),
                  _const_spec(cw.shape), _const_spec(cb.shape), _const_spec(wdn.shape)],
        out_specs=row,
        out_shape=jax.ShapeDtypeStruct((B, L, D), F32),
        scratch_shapes=[pltpu.VMEM((tm + SUBLANES, SC_WIDTH), F32),
                        pltpu.VMEM((2 * len(FFN_CHUNKS), tm + SUBLANES, max(FFN_CHUNKS)), F32)],
        compiler_params=pltpu.CompilerParams(dimension_semantics=("arbitrary", "arbitrary"),
                                             vmem_limit_bytes=VMEM_LIMIT),
        name="sc_ffn",
    )(h, mnw, win, scw, wout, nw, wup, cw, cb, wdn)


def _rope_table(seq):
    inv = 1.0 / (ROPE_THETA ** (jnp.arange(0, DA_HEAD_DIM, 2, dtype=F32) / DA_HEAD_DIM))
    ang = jnp.arange(seq, dtype=F32)[:, None] * inv[None, :]
    ang = jnp.concatenate([ang, ang], axis=-1)
    cos, sin = jnp.cos(ang), jnp.sin(ang)
    first = jnp.arange(DA_HEAD_DIM) < DA_HEAD_DIM // 2
    sin_a = jnp.where(first, -sin, 0.0)
    sin_b = jnp.where(first, 0.0, sin)
    rep = LANES // DA_HEAD_DIM
    return jnp.concatenate([jnp.tile(cos, (1, rep)), jnp.tile(sin_a, (1, rep)),
                            jnp.tile(sin_b, (1, rep))], axis=-1)


def _row(v):
    return v.reshape(1, -1).astype(F32)


def _pad_lanes(v):
    return jnp.pad(v.reshape(1, -1).astype(F32), ((0, 0), (0, LANES - v.shape[-1])))


def kernel(x, mix_norm, ffn_norm, hy_w_in, hy_conv_w, hy_conv_b, hy_dt_bias, hy_a_log, hy_d_skip,
           hy_ssd_norm, hy_q_norm, hy_k_norm, hy_lambda_q1, hy_lambda_k1, hy_lambda_q2,
           hy_lambda_k2, hy_subln, hy_w_out, sc_w_in, sc_conv_w, sc_w_out, ffn_w_up, ffn_conv_w,
           ffn_conv_b, ffn_w_down):
    B, L, D = x.shape
    tm = min(512, L)
    tq = min(2048, L)

    w = hy_w_in[0]
    o_x = SSD_D_INNER
    o_dt = o_x + SSD_CONV_DIM
    o_q = o_dt + SSD_HEADS
    o_k = o_q + DA_WIDTH
    o_v = o_k + DA_WIDTH
    wz = w[:, 0:o_x].astype(BF16)
    wx = w[:, o_x:o_dt].astype(BF16)
    wdt = jnp.pad(w[:, o_dt:o_q], ((0, 0), (0, LANES - SSD_HEADS))).astype(BF16)
    wq = w[:, o_q:o_k].astype(BF16)
    wk = w[:, o_k:o_v].astype(BF16)
    wv = w[:, o_v:o_v + DA_WIDTH].astype(BF16)
    head_of = jnp.arange(DA_WIDTH) // DA_HEAD_DIM
    g = (head_of[:, None] == jnp.arange(LANES)[None, :]).astype(F32)
    gt = jnp.concatenate([g.T, g.T], axis=0).astype(BF16)
    q_gain = jnp.tile(hy_q_norm[0], DA_WIDTH // DA_HEAD_DIM).astype(F32)
    k_gain = jnp.tile(hy_k_norm[0], DA_WIDTH // DA_HEAD_DIM).astype(F32)

    def head_sum_selector(gain):
        weighted = jnp.roll(g * (gain * gain)[:, None], HEADS_QK, axis=1)
        return (g + weighted).astype(BF16)

    rope = _rope_table(L)
    z, xc, dtf, q, k, v, q_sq, k_sq = _even_in(
        x, _row(mix_norm[0]), wz, wx, wdt, wq, wk, wv, hy_conv_w[0].astype(F32),
        _row(hy_conv_b[0]), _pad_lanes(hy_dt_bias[0]), _row(q_gain), _row(k_gain), rope,
        head_sum_selector(q_gain), head_sum_selector(k_gain), gt, tm)
    y_ssd = _ssd(xc, dtf, z, _pad_lanes(hy_a_log[0]),
                 _row(jnp.repeat(hy_d_skip[0], SSD_HEAD_DIM)), _row(hy_ssd_norm[0]), tm)
    lambda_init = 0.8 - 0.6 * math.exp(-0.3 * 0)
    y_att = _diff_attn(q, k, v, q_sq.reshape(B, DA_HEADS, 2, L), k_sq.reshape(B, DA_HEADS, 2, L),
                       _row(hy_lambda_q1[0]), _row(hy_lambda_k1[0]),
                       _row(hy_lambda_q2[0]), _row(hy_lambda_k2[0]), _row(hy_subln[0]),
                       lambda_init, tq)
    h = _out_ffn(x, y_ssd, y_att, hy_w_out[0].astype(BF16), _row(ffn_norm[0]),
                 ffn_w_up[0].astype(BF16), ffn_conv_w[0].astype(F32), _row(ffn_conv_b[0]),
                 ffn_w_down[0].astype(BF16), tm)

    h = _sc_ffn(h, _row(mix_norm[1]), sc_w_in[0].astype(BF16), sc_conv_w[0].astype(F32),
                sc_w_out[0].astype(BF16), _row(ffn_norm[1]), ffn_w_up[1].astype(BF16),
                ffn_conv_w[1].astype(F32), _row(ffn_conv_b[1]), ffn_w_down[1].astype(BF16), tm)
    return h
```

```python
import functools
import math

import jax
import jax.numpy as jnp
from jax import lax
from jax.experimental import pallas as pl
from jax.experimental.pallas import tpu as pltpu

F32 = jnp.float32
BF16 = jnp.bfloat16

RMS_EPS = 1e-6
ROPE_THETA = 10000.0
D_MODEL = 1024
SSD_HEAD_DIM = 64
SSD_HEADS = 16
SSD_GROUPS = 2
SSD_D_STATE = 128
SSD_D_INNER = 1024
SSD_CONV = 4
SSD_CHUNK = 128
SSD_CONV_DIM = SSD_D_INNER + 2 * SSD_GROUPS * SSD_D_STATE
DA_HEADS = 8
DA_HEAD_DIM = 64
HEADS_QK = 2 * DA_HEADS
EVEN_W_X = 1024
EVEN_W_DT = EVEN_W_X + 1536
EVEN_W_Q = EVEN_W_DT + 128
EVEN_W_K = EVEN_W_Q + 1024
EVEN_W_V = EVEN_W_K + 1024
EVEN_W_END = EVEN_W_V + 1024
DA_V_DIM = 128
DA_WIDTH = 1024
SC_WIDTH = 1024
SC_CONV = 3
D_FF = 2816
FFN_CONV = 3
FFN_CHUNKS = (768, 768, 768, 512)
LANES = 128
SUBLANES = 8
VMEM_LIMIT = 56 * 1024 * 1024
LOG2E = 1.4426950408889634
ATTN_KEY_BLOCK = 512
MAX_SAFE_SCORE_BOUND = 60.0


def _dot(a, b):
    return jnp.dot(a, b, preferred_element_type=F32)


def _split3(x):
    hi = x.astype(BF16)
    r1 = x - hi.astype(F32)
    mid = r1.astype(BF16)
    lo = (r1 - mid.astype(F32)).astype(BF16)
    return hi, mid, lo


def _dot_sel_rhs(x, sel):
    hi, mid, lo = _split3(x)
    return _dot(hi, sel) + _dot(mid, sel) + _dot(lo, sel)


def _dot_sel_rhs16(x, sel2):
    hi = x.astype(BF16)
    mid = (x - hi.astype(F32)).astype(BF16)
    return _dot(jnp.concatenate([hi, mid], axis=1), sel2)


def _dot_sel_lhs(sel, x):
    hi, mid, lo = _split3(x)
    return _dot(sel, hi) + _dot(sel, mid) + _dot(sel, lo)


def _rms(x, w):
    return x * lax.rsqrt(jnp.mean(x * x, axis=-1, keepdims=True) + RMS_EPS) * w


def _silu(x):
    return x * (1.0 / (1.0 + jnp.exp(-x)))


def _softplus(x):
    return jnp.maximum(x, 0.0) + jnp.log1p(jnp.exp(-jnp.abs(x)))


def _const_spec(shape):
    nd = len(shape)
    return pl.BlockSpec(shape, lambda *_: (0,) * nd, pipeline_mode=pl.Buffered(1))


def _causal_conv(buf_ref, tm, taps, first_row):
    xe = buf_ref[...]
    acc = None
    for k, w in enumerate(taps):
        back = SUBLANES - (first_row + k)
        shifted = xe if back == 0 else pltpu.roll(xe, back, 0)
        term = shifted[SUBLANES:SUBLANES + tm] * w
        acc = term if acc is None else acc + term
    return acc


def _even_in_kernel(h_ref, nw_ref, w_ref, cw_ref, cb_ref, dtb_ref, qn_ref, kn_ref, rope_ref,
                    gq_ref, gk_ref, gt_ref,
                    z_ref, xc_ref, dt_ref, q_ref, k_ref, v_ref, qsq_ref, ksq_ref, xs_ref):
    i = pl.program_id(1)
    tm = h_ref.shape[1]
    hn = _rms(h_ref[0], nw_ref[...]).astype(BF16)

    @pl.when(i == 0)
    def _():
        xs_ref[0:SUBLANES, :] = jnp.zeros((SUBLANES, SSD_CONV_DIM), F32)

    xs_ref[SUBLANES:SUBLANES + tm, :] = _dot(hn, w_ref[:, EVEN_W_X:EVEN_W_DT])
    q_raw = _dot(hn, w_ref[:, EVEN_W_Q:EVEN_W_K])

    taps = [cw_ref[k:k + 1, :] for k in range(SSD_CONV)]
    conv = _causal_conv(xs_ref, tm, taps, SUBLANES - (SSD_CONV - 1)) + cb_ref[...]
    xc_ref[0] = _silu(conv).astype(BF16)
    xs_ref[0:SUBLANES, :] = xs_ref[tm:tm + SUBLANES, :]

    k_raw = _dot(hn, w_ref[:, EVEN_W_K:EVEN_W_V])

    cos = jnp.tile(rope_ref[:, 0:LANES], (1, DA_WIDTH // LANES))
    sin_a = jnp.tile(rope_ref[:, LANES:2 * LANES], (1, DA_WIDTH // LANES))
    sin_b = jnp.tile(rope_ref[:, 2 * LANES:3 * LANES], (1, DA_WIDTH // LANES))
    half = DA_HEAD_DIM // 2

    def qk_epilogue(x, n_ref, g_ref, o_ref, sq_ref, scale):
        ss = _dot((x * x).astype(BF16), g_ref[...])
        r = lax.rsqrt(ss * (1.0 / DA_HEAD_DIM) + RMS_EPS) * scale
        xn = x * _dot_sel_rhs16(r, gt_ref[...]) * n_ref[...]
        rot = (pltpu.roll(xn, DA_WIDTH - half, 1) * sin_a + pltpu.roll(xn, half, 1) * sin_b)
        o_ref[0] = (xn * cos + rot).astype(BF16)
        nsq_t = (r * r * pltpu.roll(ss, LANES - HEADS_QK, 1)).T
        for hd in range(DA_HEADS):
            sq_ref[0, hd] = nsq_t[2 * hd:2 * hd + 2, :]

    qk_epilogue(q_raw, qn_ref, gq_ref, q_ref, qsq_ref, LOG2E * DA_HEAD_DIM ** -0.5)
    z_ref[0] = _dot(hn, w_ref[:, 0:EVEN_W_X]).astype(BF16)
    qk_epilogue(k_raw, kn_ref, gk_ref, k_ref, ksq_ref, 1.0)
    v_ref[0] = _dot(hn, w_ref[:, EVEN_W_V:EVEN_W_END]).astype(BF16)
    dt_ref[0] = _softplus(_dot(hn, w_ref[:, EVEN_W_DT:EVEN_W_Q]) + dtb_ref[...])


def _even_in(h, nw, w_in, cw, cb, dtb, qn, kn, rope, gq, gk, gt, tm):
    B, L, D = h.shape
    grid = (B, L // tm)
    row = lambda w: pl.BlockSpec((1, tm, w), lambda b, i: (b, i, 0))
    out_shape = (jax.ShapeDtypeStruct((B, L, SSD_D_INNER), BF16),
                 jax.ShapeDtypeStruct((B, L, SSD_CONV_DIM), BF16),
                 jax.ShapeDtypeStruct((B, L, LANES), F32),
                 jax.ShapeDtypeStruct((B, L, DA_WIDTH), BF16),
                 jax.ShapeDtypeStruct((B, L, DA_WIDTH), BF16),
                 jax.ShapeDtypeStruct((B, L, DA_WIDTH), BF16),
                 jax.ShapeDtypeStruct((B, DA_HEADS, 2, L), F32),
                 jax.ShapeDtypeStruct((B, DA_HEADS, 2, L), F32))
    head_rows = pl.BlockSpec((1, DA_HEADS, 2, tm), lambda b, i: (b, 0, 0, i))
    in_specs = [row(D), _const_spec(nw.shape), _const_spec(w_in.shape),
                _const_spec(cw.shape), _const_spec(cb.shape),
                _const_spec(dtb.shape), _const_spec(qn.shape), _const_spec(kn.shape),
                pl.BlockSpec((tm, 3 * LANES), lambda b, i: (i, 0)),
                _const_spec(gq.shape), _const_spec(gk.shape), _const_spec(gt.shape)]
    out_specs = (row(SSD_D_INNER), row(SSD_CONV_DIM), row(LANES), row(DA_WIDTH), row(DA_WIDTH),
                 row(DA_WIDTH), head_rows, head_rows)
    return pl.pallas_call(
        _even_in_kernel, grid=grid, in_specs=in_specs, out_specs=out_specs, out_shape=out_shape,
        scratch_shapes=[pltpu.VMEM((tm + SUBLANES, SSD_CONV_DIM), F32)],
        compiler_params=pltpu.CompilerParams(dimension_semantics=("arbitrary", "arbitrary"),
                                             vmem_limit_bytes=VMEM_LIMIT),
        name="even_in",
    )(h, nw, w_in, cw, cb, dtb, qn, kn, rope, gq, gk, gt)


def _ssd_kernel(xc_ref, dt_ref, z_ref, alog_ref, dskip_ref, nw_ref, tril_ref, expand_ref,
                y_ref, state_ref):
    T = SSD_CHUNK
    GN = SSD_GROUPS * SSD_D_STATE
    GW = SSD_D_INNER // SSD_GROUPS
    heads_per_group = SSD_HEADS // SSD_GROUPS

    @pl.when(pl.program_id(1) == 0)
    def _():
        state_ref[...] = jnp.zeros(state_ref.shape, F32)

    a = -jnp.exp(alog_ref[...])
    r_i = lax.broadcasted_iota(jnp.int32, (T, T), 0)
    c_i = lax.broadcasted_iota(jnp.int32, (T, T), 1)
    tril = r_i >= c_i
    lane = lax.broadcasted_iota(jnp.int32, (T, LANES), 1)

    for ci in range(xc_ref.shape[1] // T):
        rows = slice(ci * T, (ci + 1) * T)
        x = xc_ref[0, rows, 0:SSD_D_INNER].astype(F32)
        bm = xc_ref[0, rows, SSD_D_INNER:SSD_D_INNER + GN]
        cm = xc_ref[0, rows, SSD_D_INNER + GN:SSD_D_INNER + 2 * GN]
        dt = dt_ref[0, rows, :]
        a_cs = _dot_sel_lhs(tril_ref[...], dt * a)
        a_log2 = a_cs * LOG2E
        a_log2_t = a_log2.T
        a_last = a_cs[T - 1:T, :]

        stacked = jnp.concatenate(
            [dt, jnp.exp(a_cs), jnp.exp(a_last - a_cs),
             jnp.broadcast_to(jnp.exp(a_last), (SUBLANES, LANES))], axis=0)
        ex = _dot_sel_rhs16(stacked, expand_ref[...])
        dt_e, expa_e, decay_e = ex[0:T], ex[T:2 * T], ex[2 * T:3 * T]
        chunk_decay = ex[3 * T:3 * T + 1]
        xdt = x * dt_e

        xdt_b = xdt.astype(BF16)
        y_parts = []
        for g in range(SSD_GROUPS):
            cg = cm[:, g * SSD_D_STATE:(g + 1) * SSD_D_STATE]
            bg = bm[:, g * SSD_D_STATE:(g + 1) * SSD_D_STATE]
            cb = lax.dot_general(cg, bg, (((1,), (1,)), ((), ())), preferred_element_type=F32)
            for hp in range(heads_per_group // 2):
                h0 = g * heads_per_group + 2 * hp
                ms = []
                for h in (h0, h0 + 1):
                    seg = a_log2[:, h:h + 1] - a_log2_t[h:h + 1, :]
                    ms.append((cb * jnp.exp2(jnp.where(tril, seg, -jnp.inf))).astype(BF16))
                lhs = jnp.concatenate(ms, axis=1)
                xp = xdt_b[:, h0 * SSD_HEAD_DIM:(h0 + 2) * SSD_HEAD_DIM]
                rhs = jnp.concatenate([jnp.where(lane < SSD_HEAD_DIM, xp, 0),
                                       jnp.where(lane >= SSD_HEAD_DIM, xp, 0)], axis=0)
                y_parts.append(_dot(lhs, rhs))
        y_diag = jnp.concatenate(y_parts, axis=1)

        xdec = (xdt * decay_e).astype(BF16)
        y_off_parts = []
        for g in range(SSD_GROUPS):
            cg = cm[:, g * SSD_D_STATE:(g + 1) * SSD_D_STATE]
            y_off_parts.append(_dot(cg, state_ref[g].astype(BF16)))
            bg_t = bm[:, g * SSD_D_STATE:(g + 1) * SSD_D_STATE].astype(F32).T.astype(BF16)
            upd = _dot(bg_t, xdec[:, g * GW:(g + 1) * GW])
            state_ref[g] = state_ref[g] * chunk_decay[:, g * GW:(g + 1) * GW] + upd
        y_off = jnp.concatenate(y_off_parts, axis=1) * expa_e

        y = y_diag + y_off + dskip_ref[...] * x
        y = y * _silu(z_ref[0, rows, :].astype(F32))
        outs = []
        for g in range(SSD_GROUPS):
            yg = y[:, g * GW:(g + 1) * GW]
            outs.append(yg * lax.rsqrt(jnp.mean(yg * yg, axis=-1, keepdims=True) + RMS_EPS))
        y_ref[0, rows, :] = (jnp.concatenate(outs, axis=1) * nw_ref[...]).astype(BF16)


def _ssd(xc, dtf, z, alog, dskip, nw, ts):
    B, L, _ = xc.shape
    T = SSD_CHUNK
    grid = (B, L // ts)
    row = lambda w: pl.BlockSpec((1, ts, w), lambda b, c: (b, c, 0))
    tril = (jnp.arange(T)[:, None] >= jnp.arange(T)[None, :]).astype(BF16)
    expand = (jnp.arange(SSD_D_INNER)[None, :] // SSD_HEAD_DIM
              == jnp.arange(2 * LANES)[:, None] % LANES).astype(BF16)
    return pl.pallas_call(
        _ssd_kernel, grid=grid,
        in_specs=[row(SSD_CONV_DIM), row(LANES), row(SSD_D_INNER),
                  _const_spec(alog.shape), _const_spec(dskip.shape), _const_spec(nw.shape),
                  _const_spec(tril.shape), _const_spec(expand.shape)],
        out_specs=row(SSD_D_INNER),
        out_shape=jax.ShapeDtypeStruct((B, L, SSD_D_INNER), BF16),
        scratch_shapes=[pltpu.VMEM((SSD_GROUPS, SSD_D_STATE, SSD_D_INNER // SSD_GROUPS), F32)],
        compiler_params=pltpu.CompilerParams(dimension_semantics=("arbitrary", "arbitrary"),
                                             vmem_limit_bytes=VMEM_LIMIT),
        name="ssd",
    )(xc, dtf, z, alog, dskip, nw, tril, expand)


def _attn_kernel(q_ref, k_ref, v_ref, qsq_ref, ksq_ref, lq1_ref, lk1_ref, lq2_ref, lk2_ref, sub_ref,
                 o_ref, q2_ref, *, tq, lambda_init):
    i = pl.program_id(2)
    tk = ATTN_KEY_BLOCK
    nq = tq // tk
    q = q_ref[0]
    lane = lax.broadcasted_iota(jnp.int32, q.shape, 1)
    zero = jnp.zeros_like(q)
    q_map1 = jnp.where(lane < DA_HEAD_DIM, q, zero)
    q_map2 = jnp.where(lane >= DA_HEAD_DIM, q, zero)
    for g in range(nq):
        q2_ref[2 * g * tk:(2 * g + 1) * tk, :] = q_map1[g * tk:(g + 1) * tk]
        q2_ref[(2 * g + 1) * tk:(2 * g + 2) * tk, :] = q_map2[g * tk:(g + 1) * tk]

    def from_group(g):
        return slice(2 * g * tk, 2 * tq)

    def scores(j, first_group=0):
        kb = k_ref[0, pl.ds(pl.multiple_of(j * tk, tk), tk), :]
        return lax.dot_general(kb, q2_ref[from_group(first_group), :], (((1,), (1,)), ((), ())),
                               preferred_element_type=F32)

    def causal(s):
        kpos = lax.broadcasted_iota(jnp.int32, s.shape, 0)
        col = lax.broadcasted_iota(jnp.int32, s.shape, 1)
        qpos = jnp.where(col >= tk, col - tk, col)
        return jnp.where(kpos <= qpos, s, -jnp.inf)

    def diagonal_scores(g):
        s = scores(nq * i + g, first_group=g)
        if g + 1 == nq:
            return causal(s)
        return jnp.concatenate([causal(s[:, 0:2 * tk]), s[:, 2 * tk:]], axis=1)

    def merge(full, part, g):
        return part if g == 0 else jnp.concatenate([full[:, 0:2 * g * tk], part], axis=1)

    def v_block(j):
        return v_ref[0, pl.ds(pl.multiple_of(j * tk, tk), tk), :]

    def update(j, s, carry):
        m, l, acc = carry
        m_new = jnp.maximum(m, jnp.max(s, axis=0, keepdims=True))
        p = jnp.exp2(s - m_new)
        alpha = jnp.exp2(m - m_new)
        l_new = alpha * l + jnp.sum(p, axis=0, keepdims=True)
        pv = lax.dot_general(v_block(j), p.astype(BF16), (((0,), (0,)), ((), ())),
                             preferred_element_type=F32)
        return m_new, l_new, alpha * acc + pv

    def pair(j0, carry):
        s_a = scores(j0)
        s_b = scores(j0 + 1)
        return update(j0 + 1, s_b, update(j0, s_a, carry))

    def online_path():
        init = (jnp.full((1, 2 * tq), -jnp.inf, F32), jnp.zeros((1, 2 * tq), F32),
                jnp.zeros((DA_V_DIM, 2 * tq), F32))
        m, l, acc = lax.fori_loop(0, (nq // 2) * i, lambda t, c: pair(2 * t, c), init)
        for g in range(nq):
            cols = from_group(g)
            part = update(nq * i + g, diagonal_scores(g), (m[:, cols], l[:, cols], acc[:, cols]))
            m, l, acc = merge(m, part[0], g), merge(l, part[1], g), merge(acc, part[2], g)
        return l, acc

    qsq = qsq_ref[0, 0]
    kmax = jnp.max(ksq_ref[0, 0], axis=1, keepdims=True)
    bound = jnp.sqrt(jnp.concatenate(
        [qsq[mp:mp + 1, g * tk:(g + 1) * tk] * kmax[mp:mp + 1]
         for g in range(nq) for mp in range(2)], axis=1))

    def bounded_update(j, s, shift, carry):
        l, acc = carry
        p = jnp.exp2(s - shift)
        pv = lax.dot_general(v_block(j), p.astype(BF16), (((0,), (0,)), ((), ())),
                             preferred_element_type=F32)
        return l + jnp.sum(p, axis=0, keepdims=True), acc + pv

    def bounded_pair(j0, carry):
        carry = bounded_update(j0, scores(j0), bound, carry)
        return bounded_update(j0 + 1, scores(j0 + 1), bound, carry)

    def bounded_path():
        init = (jnp.zeros((1, 2 * tq), F32), jnp.zeros((DA_V_DIM, 2 * tq), F32))
        l, acc = lax.fori_loop(0, (nq // 2) * i, lambda t, c: bounded_pair(2 * t, c), init)
        for g in range(nq):
            cols = from_group(g)
            part = bounded_update(nq * i + g, diagonal_scores(g), bound[:, cols],
                                  (l[:, cols], acc[:, cols]))
            l, acc = merge(l, part[0], g), merge(acc, part[1], g)
        return l, acc

    l, acc = lax.cond(jnp.max(bound) <= MAX_SAFE_SCORE_BOUND, bounded_path, online_path)
    o_t = acc * (1.0 / l)

    lam = (jnp.exp(jnp.sum(lq1_ref[...] * lk1_ref[...], axis=-1, keepdims=True))
           - jnp.exp(jnp.sum(lq2_ref[...] * lk2_ref[...], axis=-1, keepdims=True)) + lambda_init)
    o_map1 = jnp.concatenate([o_t[:, 2 * g * tk:(2 * g + 1) * tk] for g in range(nq)], axis=1)
    o_map2 = jnp.concatenate([o_t[:, (2 * g + 1) * tk:(2 * g + 2) * tk] for g in range(nq)], axis=1)
    d = o_map1 - lam * o_map2
    scale = lax.rsqrt(jnp.mean(d * d, axis=0, keepdims=True) + RMS_EPS) * (1.0 - lambda_init)
    o_ref[0] = ((d * scale).T * sub_ref[...]).astype(BF16)


def _diff_attn(q, k, v, q_sq, k_sq, lq1, lk1, lq2, lk2, sub, lambda_init, tq):
    B, L, _ = q.shape
    grid = (B, DA_HEADS, L // tq)
    full = pl.BlockSpec((1, L, DA_V_DIM), lambda b, h, i: (b, 0, h))
    blk = pl.BlockSpec((1, tq, DA_V_DIM), lambda b, h, i: (b, i, h))
    return pl.pallas_call(
        functools.partial(_attn_kernel, tq=tq, lambda_init=lambda_init),
        grid=grid,
        scratch_shapes=[pltpu.VMEM((2 * tq, DA_V_DIM), BF16)],
        in_specs=[blk, full, full,
                  pl.BlockSpec((1, 1, 2, tq), lambda b, h, i: (b, h, 0, i)),
                  pl.BlockSpec((1, 1, 2, L), lambda b, h, i: (b, h, 0, 0)),
                  _const_spec(lq1.shape), _const_spec(lk1.shape),
                  _const_spec(lq2.shape), _const_spec(lk2.shape), _const_spec(sub.shape)],
        out_specs=blk,
        out_shape=jax.ShapeDtypeStruct((B, L, DA_WIDTH), BF16),
        compiler_params=pltpu.CompilerParams(
            dimension_semantics=("arbitrary", "arbitrary", "arbitrary"),
            vmem_limit_bytes=VMEM_LIMIT),
        name="diff_attn",
    )(q, k, v, q_sq, k_sq, lq1, lk1, lq2, lk2, sub)


def _ffn_tile(o_ref, i, nw_ref, wup_ref, cw_ref, cb_ref, wdn_ref, buf_ref, tm):
    hn = _rms(o_ref[0], nw_ref[...]).astype(BF16)
    lead = SUBLANES - (FFN_CONV - 1)

    @pl.when(i == 0)
    def _():
        buf_ref[:, 0:SUBLANES, :] = jnp.zeros((buf_ref.shape[0], SUBLANES, buf_ref.shape[2]), F32)

    starts = [sum(FFN_CHUNKS[:c]) for c in range(len(FFN_CHUNKS))]

    def slot(c, part):
        return buf_ref.at[2 * c + part, :, 0:FFN_CHUNKS[c]]

    def up_project(c):
        for part in range(2):
            col = part * D_FF + starts[c]
            slot(c, part)[SUBLANES:SUBLANES + tm, :] = _dot(hn, wup_ref[:, col:col + FFN_CHUNKS[c]])

    up_project(0)
    for c, width in enumerate(FFN_CHUNKS):
        if c + 1 < len(FFN_CHUNKS):
            up_project(c + 1)
        acts = []
        for part in range(2):
            col = part * D_FF + starts[c]
            taps = [cw_ref[k:k + 1, col:col + width] for k in range(FFN_CONV)]
            acts.append(_causal_conv(slot(c, part), tm, taps, lead) + cb_ref[:, col:col + width])
            slot(c, part)[0:SUBLANES, :] = slot(c, part)[tm:tm + SUBLANES, :]
        act = (_silu(acts[0]) * acts[1]).astype(BF16)
        o_ref[0] += _dot(act, wdn_ref[starts[c]:starts[c] + width, :])


def _out_ffn_kernel(h_ref, ys_ref, ya_ref, wo_ref, nw_ref, wup_ref, cw_ref, cb_ref, wdn_ref,
                    o_ref, buf_ref):
    i = pl.program_id(1)
    tm = h_ref.shape[1]
    o_ref[0] = (h_ref[0] + _dot(ys_ref[0], wo_ref[0:SSD_D_INNER, :])
                + _dot(ya_ref[0], wo_ref[SSD_D_INNER:SSD_D_INNER + DA_WIDTH, :]))
    _ffn_tile(o_ref, i, nw_ref, wup_ref, cw_ref, cb_ref, wdn_ref, buf_ref, tm)


def _out_ffn(h, ys, ya, wo, nw, wup, cw, cb, wdn, tm):
    B, L, D = h.shape
    grid = (B, L // tm)
    row = lambda w: pl.BlockSpec((1, tm, w), lambda b, i: (b, i, 0))
    return pl.pallas_call(
        _out_ffn_kernel, grid=grid,
        in_specs=[row(D), row(SSD_D_INNER), row(DA_WIDTH), _const_spec(wo.shape),
                  _const_spec(nw.shape), _const_spec(wup.shape), _const_spec(cw.shape),
                  _const_spec(cb.shape), _const_spec(wdn.shape)],
        out_specs=row(D),
        out_shape=jax.ShapeDtypeStruct((B, L, D), F32),
        scratch_shapes=[pltpu.VMEM((2 * len(FFN_CHUNKS), tm + SUBLANES, max(FFN_CHUNKS)), F32)],
        compiler_params=pltpu.CompilerParams(dimension_semantics=("arbitrary", "arbitrary"),
                                             vmem_limit_bytes=VMEM_LIMIT),
        name="out_ffn",
    )(h, ys, ya, wo, nw, wup, cw, cb, wdn)


def _sc_ffn_kernel(h_ref, mnw_ref, win_ref, scw_ref, wout_ref, nw_ref, wup_ref, cw_ref, cb_ref,
                   wdn_ref, o_ref, mbuf_ref, buf_ref):
    i = pl.program_id(1)
    tm = h_ref.shape[1]
    h = h_ref[0]
    hn = _rms(h, mnw_ref[...]).astype(BF16)

    @pl.when(i == 0)
    def _():
        mbuf_ref[0:SUBLANES, :] = jnp.zeros((SUBLANES, SC_WIDTH), F32)

    cg = _dot(hn, win_ref[:, SC_WIDTH:2 * SC_WIDTH])
    u = _dot(hn, win_ref[:, 2 * SC_WIDTH:3 * SC_WIDTH])
    bg = _dot(hn, win_ref[:, 0:SC_WIDTH])
    mbuf_ref[SUBLANES:SUBLANES + tm, :] = cg * u
    taps = [scw_ref[k:k + 1, :] for k in range(SC_CONV)]
    conv = _causal_conv(mbuf_ref, tm, taps, SUBLANES - (SC_CONV - 1))
    mbuf_ref[0:SUBLANES, :] = mbuf_ref[tm:tm + SUBLANES, :]
    o_ref[0] = h + _dot((bg * conv).astype(BF16), wout_ref[...])
    _ffn_tile(o_ref, i, nw_ref, wup_ref, cw_ref, cb_ref, wdn_ref, buf_ref, tm)


def _sc_ffn(h, mnw, win, scw, wout, nw, wup, cw, cb, wdn, tm):
    B, L, D = h.shape
    grid = (B, L // tm)
    row = pl.BlockSpec((1, tm, D), lambda b, i: (b, i, 0))
    return pl.pallas_call(
        _sc_ffn_kernel, grid=grid,
        in_specs=[row, _const_spec(mnw.shape), _const_spec(win.shape), _const_spec(scw.shape),
                  _const_spec(wout.shape), _const_spec(nw.shape), _const_spec(wup.shape),
                  _const_spec(cw.shape), _const_spec(cb.shape), _const_spec(wdn.shape)],
        out_specs=row,
        out_shape=jax.ShapeDtypeStruct((B, L, D), F32),
        scratch_shapes=[pltpu.VMEM((tm + SUBLANES, SC_WIDTH), F32),
                        pltpu.VMEM((2 * len(FFN_CHUNKS), tm + SUBLANES, max(FFN_CHUNKS)), F32)],
        compiler_params=pltpu.CompilerParams(dimension_semantics=("arbitrary", "arbitrary"),
                                             vmem_limit_bytes=VMEM_LIMIT),
        name="sc_ffn",
    )(h, mnw, win, scw, wout, nw, wup, cw, cb, wdn)


def _rope_table(seq):
    inv = 1.0 / (ROPE_THETA ** (jnp.arange(0, DA_HEAD_DIM, 2, dtype=F32) / DA_HEAD_DIM))
    ang = jnp.arange(seq, dtype=F32)[:, None] * inv[None, :]
    ang = jnp.concatenate([ang, ang], axis=-1)
    cos, sin = jnp.cos(ang), jnp.sin(ang)
    first = jnp.arange(DA_HEAD_DIM) < DA_HEAD_DIM // 2
    sin_a = jnp.where(first, -sin, 0.0)
    sin_b = jnp.where(first, 0.0, sin)
    rep = LANES // DA_HEAD_DIM
    return jnp.concatenate([jnp.tile(cos, (1, rep)), jnp.tile(sin_a, (1, rep)),
                            jnp.tile(sin_b, (1, rep))], axis=-1)


def _row(v):
    return v.reshape(1, -1).astype(F32)


def _pad_lanes(v):
    return jnp.pad(v.reshape(1, -1).astype(F32), ((0, 0), (0, LANES - v.shape[-1])))


def kernel(x, mix_norm, ffn_norm, hy_w_in, hy_conv_w, hy_conv_b, hy_dt_bias, hy_a_log, hy_d_skip,
           hy_ssd_norm, hy_q_norm, hy_k_norm, hy_lambda_q1, hy_lambda_k1, hy_lambda_q2,
           hy_lambda_k2, hy_subln, hy_w_out, sc_w_in, sc_conv_w, sc_w_out, ffn_w_up, ffn_conv_w,
           ffn_conv_b, ffn_w_down):
    B, L, D = x.shape
    tm = min(512, L)
    tq = min(2048, L)

    w = hy_w_in[0]
    o_x = SSD_D_INNER
    o_dt = o_x + SSD_CONV_DIM
    o_q = o_dt + SSD_HEADS
    o_k = o_q + DA_WIDTH
    o_v = o_k + DA_WIDTH
    w_in = jnp.concatenate(
        [w[:, 0:o_dt].astype(BF16),
         jnp.pad(w[:, o_dt:o_q], ((0, 0), (0, LANES - SSD_HEADS))).astype(BF16),
         w[:, o_q:o_v + DA_WIDTH].astype(BF16)], axis=1)
    head_of = jnp.arange(DA_WIDTH) // DA_HEAD_DIM
    g = (head_of[:, None] == jnp.arange(LANES)[None, :]).astype(F32)
    gt = jnp.concatenate([g.T, g.T], axis=0).astype(BF16)
    q_gain = jnp.tile(hy_q_norm[0], DA_WIDTH // DA_HEAD_DIM).astype(F32)
    k_gain = jnp.tile(hy_k_norm[0], DA_WIDTH // DA_HEAD_DIM).astype(F32)

    def head_sum_selector(gain):
        weighted = jnp.roll(g * (gain * gain)[:, None], HEADS_QK, axis=1)
        return (g + weighted).astype(BF16)

    rope = _rope_table(L)
    z, xc, dtf, q, k, v, q_sq, k_sq = _even_in(
        x, _row(mix_norm[0]), w_in, hy_conv_w[0].astype(F32),
        _row(hy_conv_b[0]), _pad_lanes(hy_dt_bias[0]), _row(q_gain), _row(k_gain), rope,
        head_sum_selector(q_gain), head_sum_selector(k_gain), gt, tm)
    y_ssd = _ssd(xc, dtf, z, _pad_lanes(hy_a_log[0]),
                 _row(jnp.repeat(hy_d_skip[0], SSD_HEAD_DIM)), _row(hy_ssd_norm[0]), tm)
    lambda_init = 0.8 - 0.6 * math.exp(-0.3 * 0)
    y_att = _diff_attn(q, k, v, q_sq, k_sq,
                       _row(hy_lambda_q1[0]), _row(hy_lambda_k1[0]),
                       _row(hy_lambda_q2[0]), _row(hy_lambda_k2[0]), _row(hy_subln[0]),
                       lambda_init, tq)
    h = _out_ffn(x, y_ssd, y_att, hy_w_out[0].astype(BF16), _row(ffn_norm[0]),
                 ffn_w_up[0].astype(BF16), ffn_conv_w[0].astype(F32), _row(ffn_conv_b[0]),
                 ffn_w_down[0].astype(BF16), tm)

    h = _sc_ffn(h, _row(mix_norm[1]), sc_w_in[0].astype(BF16), sc_conv_w[0].astype(F32),
                sc_w_out[0].astype(BF16), _row(ffn_norm[1]), ffn_w_up[1].astype(BF16),
                ffn_conv_w[1].astype(F32), _row(ffn_conv_b[1]), ffn_w_down[1].astype(BF16), tm)
    return h
```

```python
import functools
import math

import jax
import jax.numpy as jnp
from jax import lax
from jax.experimental import pallas as pl
from jax.experimental.pallas import tpu as pltpu

F32 = jnp.float32
BF16 = jnp.bfloat16

RMS_EPS = 1e-6
ROPE_THETA = 10000.0
D_MODEL = 1024
SSD_HEAD_DIM = 64
SSD_HEADS = 16
SSD_GROUPS = 2
SSD_D_STATE = 128
SSD_D_INNER = 1024
SSD_CONV = 4
SSD_CHUNK = 128
SSD_CONV_DIM = SSD_D_INNER + 2 * SSD_GROUPS * SSD_D_STATE
DA_HEADS = 8
DA_HEAD_DIM = 64
HEADS_QK = 2 * DA_HEADS
EVEN_W_X = 1024
EVEN_W_DT = EVEN_W_X + 1536
EVEN_W_Q = EVEN_W_DT + 128
EVEN_W_K = EVEN_W_Q + 1024
EVEN_W_V = EVEN_W_K + 1024
EVEN_W_END = EVEN_W_V + 1024
DA_V_DIM = 128
DA_WIDTH = 1024
SC_WIDTH = 1024
SC_CONV = 3
D_FF = 2816
FFN_CONV = 3
FFN_CHUNKS = (768, 768, 768, 512)
LANES = 128
SUBLANES = 8
VMEM_LIMIT = 56 * 1024 * 1024
LOG2E = 1.4426950408889634
ATTN_KEY_BLOCK = 512
MAX_SAFE_SCORE_BOUND = 60.0


def _dot(a, b):
    return jnp.dot(a, b, preferred_element_type=F32)


def _split3(x):
    hi = x.astype(BF16)
    r1 = x - hi.astype(F32)
    mid = r1.astype(BF16)
    lo = (r1 - mid.astype(F32)).astype(BF16)
    return hi, mid, lo


def _dot_sel_rhs(x, sel):
    hi, mid, lo = _split3(x)
    return _dot(hi, sel) + _dot(mid, sel) + _dot(lo, sel)


def _dot_sel_rhs16(x, sel2):
    hi = x.astype(BF16)
    mid = (x - hi.astype(F32)).astype(BF16)
    return _dot(jnp.concatenate([hi, mid], axis=1), sel2)


def _dot_sel_lhs(sel, x):
    hi, mid, lo = _split3(x)
    return _dot(sel, hi) + _dot(sel, mid) + _dot(sel, lo)


def _rms(x, w):
    return x * lax.rsqrt(jnp.mean(x * x, axis=-1, keepdims=True) + RMS_EPS) * w


def _silu(x):
    return x * (1.0 / (1.0 + jnp.exp(-x)))


def _softplus(x):
    return jnp.maximum(x, 0.0) + jnp.log1p(jnp.exp(-jnp.abs(x)))


def _const_spec(shape):
    nd = len(shape)
    return pl.BlockSpec(shape, lambda *_: (0,) * nd, pipeline_mode=pl.Buffered(1))


def _layer_spec(stacked_shape, layer):
    _, rows, cols = stacked_shape
    return pl.BlockSpec((None, rows, cols), lambda *_: (layer, 0, 0), pipeline_mode=pl.Buffered(1))


def _causal_conv(buf_ref, tm, taps, first_row):
    xe = buf_ref[...]
    acc = None
    for k, w in enumerate(taps):
        back = SUBLANES - (first_row + k)
        shifted = xe if back == 0 else pltpu.roll(xe, back, 0)
        term = shifted[SUBLANES:SUBLANES + tm] * w
        acc = term if acc is None else acc + term
    return acc


def _even_in_kernel(h_ref, nw_ref, w_ref, cw_ref, cb_ref, dtb_ref, qn_ref, kn_ref, rope_ref,
                    gq_ref, gk_ref, gt_ref,
                    z_ref, xc_ref, dt_ref, q_ref, k_ref, v_ref, qsq_ref, ksq_ref, xs_ref):
    i = pl.program_id(1)
    tm = h_ref.shape[1]
    hn = _rms(h_ref[0], nw_ref[...]).astype(BF16)

    @pl.when(i == 0)
    def _():
        xs_ref[0:SUBLANES, :] = jnp.zeros((SUBLANES, SSD_CONV_DIM), F32)

    xs_ref[SUBLANES:SUBLANES + tm, :] = _dot(hn, w_ref[:, EVEN_W_X:EVEN_W_DT])
    q_raw = _dot(hn, w_ref[:, EVEN_W_Q:EVEN_W_K])

    taps = [cw_ref[k:k + 1, :] for k in range(SSD_CONV)]
    conv = _causal_conv(xs_ref, tm, taps, SUBLANES - (SSD_CONV - 1)) + cb_ref[...]
    xc_ref[0] = _silu(conv).astype(BF16)
    xs_ref[0:SUBLANES, :] = xs_ref[tm:tm + SUBLANES, :]

    k_raw = _dot(hn, w_ref[:, EVEN_W_K:EVEN_W_V])

    cos = jnp.tile(rope_ref[:, 0:LANES], (1, DA_WIDTH // LANES))
    sin_a = jnp.tile(rope_ref[:, LANES:2 * LANES], (1, DA_WIDTH // LANES))
    sin_b = jnp.tile(rope_ref[:, 2 * LANES:3 * LANES], (1, DA_WIDTH // LANES))
    half = DA_HEAD_DIM // 2

    def qk_epilogue(x, n_ref, g_ref, o_ref, sq_ref, scale):
        ss = _dot((x * x).astype(BF16), g_ref[...])
        r = lax.rsqrt(ss * (1.0 / DA_HEAD_DIM) + RMS_EPS) * scale
        xn = x * _dot_sel_rhs16(r, gt_ref[...]) * n_ref[...]
        rot = (pltpu.roll(xn, DA_WIDTH - half, 1) * sin_a + pltpu.roll(xn, half, 1) * sin_b)
        o_ref[0] = (xn * cos + rot).astype(BF16)
        nsq_t = (r * r * pltpu.roll(ss, LANES - HEADS_QK, 1)).T
        for hd in range(DA_HEADS):
            sq_ref[0, hd] = nsq_t[2 * hd:2 * hd + 2, :]

    qk_epilogue(q_raw, qn_ref, gq_ref, q_ref, qsq_ref, LOG2E * DA_HEAD_DIM ** -0.5)
    z_ref[0] = _dot(hn, w_ref[:, 0:EVEN_W_X]).astype(BF16)
    qk_epilogue(k_raw, kn_ref, gk_ref, k_ref, ksq_ref, 1.0)
    v_ref[0] = _dot(hn, w_ref[:, EVEN_W_V:EVEN_W_END]).astype(BF16)
    dt_ref[0] = _softplus(_dot(hn, w_ref[:, EVEN_W_DT:EVEN_W_Q]) + dtb_ref[...])


def _even_in(h, nw, w_in, cw, cb, dtb, qn, kn, rope, gq, gk, gt, tm):
    B, L, D = h.shape
    grid = (B, L // tm)
    row = lambda w: pl.BlockSpec((1, tm, w), lambda b, i: (b, i, 0))
    out_shape = (jax.ShapeDtypeStruct((B, L, SSD_D_INNER), BF16),
                 jax.ShapeDtypeStruct((B, L, SSD_CONV_DIM), BF16),
                 jax.ShapeDtypeStruct((B, L, LANES), F32),
                 jax.ShapeDtypeStruct((B, L, DA_WIDTH), BF16),
                 jax.ShapeDtypeStruct((B, L, DA_WIDTH), BF16),
                 jax.ShapeDtypeStruct((B, L, DA_WIDTH), BF16),
                 jax.ShapeDtypeStruct((B, DA_HEADS, 2, L), F32),
                 jax.ShapeDtypeStruct((B, DA_HEADS, 2, L), F32))
    head_rows = pl.BlockSpec((1, DA_HEADS, 2, tm), lambda b, i: (b, 0, 0, i))
    in_specs = [row(D), _const_spec(nw.shape), _const_spec(w_in.shape),
                _const_spec(cw.shape), _const_spec(cb.shape),
                _const_spec(dtb.shape), _const_spec(qn.shape), _const_spec(kn.shape),
                pl.BlockSpec((tm, 3 * LANES), lambda b, i: (i, 0)),
                _const_spec(gq.shape), _const_spec(gk.shape), _const_spec(gt.shape)]
    out_specs = (row(SSD_D_INNER), row(SSD_CONV_DIM), row(LANES), row(DA_WIDTH), row(DA_WIDTH),
                 row(DA_WIDTH), head_rows, head_rows)
    return pl.pallas_call(
        _even_in_kernel, grid=grid, in_specs=in_specs, out_specs=out_specs, out_shape=out_shape,
        scratch_shapes=[pltpu.VMEM((tm + SUBLANES, SSD_CONV_DIM), F32)],
        compiler_params=pltpu.CompilerParams(dimension_semantics=("arbitrary", "arbitrary"),
                                             vmem_limit_bytes=VMEM_LIMIT),
        name="even_in",
    )(h, nw, w_in, cw, cb, dtb, qn, kn, rope, gq, gk, gt)


def _ssd_kernel(xc_ref, dt_ref, z_ref, alog_ref, dskip_ref, nw_ref, tril_ref, expand_ref,
                y_ref, state_ref):
    T = SSD_CHUNK
    GN = SSD_GROUPS * SSD_D_STATE
    GW = SSD_D_INNER // SSD_GROUPS
    heads_per_group = SSD_HEADS // SSD_GROUPS

    @pl.when(pl.program_id(1) == 0)
    def _():
        state_ref[...] = jnp.zeros(state_ref.shape, F32)

    a = -jnp.exp(alog_ref[...])
    r_i = lax.broadcasted_iota(jnp.int32, (T, T), 0)
    c_i = lax.broadcasted_iota(jnp.int32, (T, T), 1)
    tril = r_i >= c_i
    lane = lax.broadcasted_iota(jnp.int32, (T, LANES), 1)

    for ci in range(xc_ref.shape[1] // T):
        rows = slice(ci * T, (ci + 1) * T)
        x = xc_ref[0, rows, 0:SSD_D_INNER].astype(F32)
        bm = xc_ref[0, rows, SSD_D_INNER:SSD_D_INNER + GN]
        cm = xc_ref[0, rows, SSD_D_INNER + GN:SSD_D_INNER + 2 * GN]
        dt = dt_ref[0, rows, :]
        a_cs = _dot_sel_lhs(tril_ref[...], dt * a)
        a_log2 = a_cs * LOG2E
        a_log2_t = a_log2.T
        a_last = a_cs[T - 1:T, :]

        stacked = jnp.concatenate(
            [dt, jnp.exp(a_cs), jnp.exp(a_last - a_cs),
             jnp.broadcast_to(jnp.exp(a_last), (SUBLANES, LANES))], axis=0)
        ex = _dot_sel_rhs16(stacked, expand_ref[...])
        dt_e, expa_e, decay_e = ex[0:T], ex[T:2 * T], ex[2 * T:3 * T]
        chunk_decay = ex[3 * T:3 * T + 1]
        xdt = x * dt_e

        xdt_b = xdt.astype(BF16)
        y_parts = []
        for g in range(SSD_GROUPS):
            cg = cm[:, g * SSD_D_STATE:(g + 1) * SSD_D_STATE]
            bg = bm[:, g * SSD_D_STATE:(g + 1) * SSD_D_STATE]
            cb = lax.dot_general(cg, bg, (((1,), (1,)), ((), ())), preferred_element_type=F32)
            for hp in range(heads_per_group // 2):
                h0 = g * heads_per_group + 2 * hp
                ms = []
                for h in (h0, h0 + 1):
                    seg = a_log2[:, h:h + 1] - a_log2_t[h:h + 1, :]
                    ms.append((cb * jnp.exp2(jnp.where(tril, seg, -jnp.inf))).astype(BF16))
                lhs = jnp.concatenate(ms, axis=1)
                xp = xdt_b[:, h0 * SSD_HEAD_DIM:(h0 + 2) * SSD_HEAD_DIM]
                rhs = jnp.concatenate([jnp.where(lane < SSD_HEAD_DIM, xp, 0),
                                       jnp.where(lane >= SSD_HEAD_DIM, xp, 0)], axis=0)
                y_parts.append(_dot(lhs, rhs))
        y_diag = jnp.concatenate(y_parts, axis=1)

        xdec = (xdt * decay_e).astype(BF16)
        y_off_parts = []
        for g in range(SSD_GROUPS):
            cg = cm[:, g * SSD_D_STATE:(g + 1) * SSD_D_STATE]
            y_off_parts.append(_dot(cg, state_ref[g].astype(BF16)))
            bg_t = bm[:, g * SSD_D_STATE:(g + 1) * SSD_D_STATE].astype(F32).T.astype(BF16)
            upd = _dot(bg_t, xdec[:, g * GW:(g + 1) * GW])
            state_ref[g] = state_ref[g] * chunk_decay[:, g * GW:(g + 1) * GW] + upd
        y_off = jnp.concatenate(y_off_parts, axis=1) * expa_e

        y = y_diag + y_off + dskip_ref[...] * x
        y = y * _silu(z_ref[0, rows, :].astype(F32))
        outs = []
        for g in range(SSD_GROUPS):
            yg = y[:, g * GW:(g + 1) * GW]
            outs.append(yg * lax.rsqrt(jnp.mean(yg * yg, axis=-1, keepdims=True) + RMS_EPS))
        y_ref[0, rows, :] = (jnp.concatenate(outs, axis=1) * nw_ref[...]).astype(BF16)


def _ssd(xc, dtf, z, alog, dskip, nw, ts):
    B, L, _ = xc.shape
    T = SSD_CHUNK
    grid = (B, L // ts)
    row = lambda w: pl.BlockSpec((1, ts, w), lambda b, c: (b, c, 0))
    tril = (jnp.arange(T)[:, None] >= jnp.arange(T)[None, :]).astype(BF16)
    expand = (jnp.arange(SSD_D_INNER)[None, :] // SSD_HEAD_DIM
              == jnp.arange(2 * LANES)[:, None] % LANES).astype(BF16)
    return pl.pallas_call(
        _ssd_kernel, grid=grid,
        in_specs=[row(SSD_CONV_DIM), row(LANES), row(SSD_D_INNER),
                  _const_spec(alog.shape), _const_spec(dskip.shape), _const_spec(nw.shape),
                  _const_spec(tril.shape), _const_spec(expand.shape)],
        out_specs=row(SSD_D_INNER),
        out_shape=jax.ShapeDtypeStruct((B, L, SSD_D_INNER), BF16),
        scratch_shapes=[pltpu.VMEM((SSD_GROUPS, SSD_D_STATE, SSD_D_INNER // SSD_GROUPS), F32)],
        compiler_params=pltpu.CompilerParams(dimension_semantics=("arbitrary", "arbitrary"),
                                             vmem_limit_bytes=VMEM_LIMIT),
        name="ssd",
    )(xc, dtf, z, alog, dskip, nw, tril, expand)


def _attn_kernel(q_ref, k_ref, v_ref, qsq_ref, ksq_ref, lq1_ref, lk1_ref, lq2_ref, lk2_ref, sub_ref,
                 o_ref, q2_ref, *, tq, lambda_init):
    i = pl.program_id(2)
    tk = ATTN_KEY_BLOCK
    nq = tq // tk
    q = q_ref[0]
    lane = lax.broadcasted_iota(jnp.int32, q.shape, 1)
    zero = jnp.zeros_like(q)
    q_map1 = jnp.where(lane < DA_HEAD_DIM, q, zero)
    q_map2 = jnp.where(lane >= DA_HEAD_DIM, q, zero)
    for g in range(nq):
        q2_ref[2 * g * tk:(2 * g + 1) * tk, :] = q_map1[g * tk:(g + 1) * tk]
        q2_ref[(2 * g + 1) * tk:(2 * g + 2) * tk, :] = q_map2[g * tk:(g + 1) * tk]

    def from_group(g):
        return slice(2 * g * tk, 2 * tq)

    def scores(j, first_group=0):
        kb = k_ref[0, pl.ds(pl.multiple_of(j * tk, tk), tk), :]
        return lax.dot_general(kb, q2_ref[from_group(first_group), :], (((1,), (1,)), ((), ())),
                               preferred_element_type=F32)

    def causal(s):
        kpos = lax.broadcasted_iota(jnp.int32, s.shape, 0)
        col = lax.broadcasted_iota(jnp.int32, s.shape, 1)
        qpos = jnp.where(col >= tk, col - tk, col)
        return jnp.where(kpos <= qpos, s, -jnp.inf)

    def diagonal_scores(g):
        s = scores(nq * i + g, first_group=g)
        if g + 1 == nq:
            return causal(s)
        return jnp.concatenate([causal(s[:, 0:2 * tk]), s[:, 2 * tk:]], axis=1)

    def merge(full, part, g):
        return part if g == 0 else jnp.concatenate([full[:, 0:2 * g * tk], part], axis=1)

    def v_block(j):
        return v_ref[0, pl.ds(pl.multiple_of(j * tk, tk), tk), :]

    def update(j, s, carry):
        m, l, acc = carry
        m_new = jnp.maximum(m, jnp.max(s, axis=0, keepdims=True))
        p = jnp.exp2(s - m_new)
        alpha = jnp.exp2(m - m_new)
        l_new = alpha * l + jnp.sum(p, axis=0, keepdims=True)
        pv = lax.dot_general(v_block(j), p.astype(BF16), (((0,), (0,)), ((), ())),
                             preferred_element_type=F32)
        return m_new, l_new, alpha * acc + pv

    def pair(j0, carry):
        s_a = scores(j0)
        s_b = scores(j0 + 1)
        return update(j0 + 1, s_b, update(j0, s_a, carry))

    def online_path():
        init = (jnp.full((1, 2 * tq), -jnp.inf, F32), jnp.zeros((1, 2 * tq), F32),
                jnp.zeros((DA_V_DIM, 2 * tq), F32))
        m, l, acc = lax.fori_loop(0, (nq // 2) * i, lambda t, c: pair(2 * t, c), init)
        for g in range(nq):
            cols = from_group(g)
            part = update(nq * i + g, diagonal_scores(g), (m[:, cols], l[:, cols], acc[:, cols]))
            m, l, acc = merge(m, part[0], g), merge(l, part[1], g), merge(acc, part[2], g)
        return l, acc

    qsq = qsq_ref[0, 0]
    kmax = jnp.max(ksq_ref[0, 0], axis=1, keepdims=True)
    bound = jnp.sqrt(jnp.concatenate(
        [qsq[mp:mp + 1, g * tk:(g + 1) * tk] * kmax[mp:mp + 1]
         for g in range(nq) for mp in range(2)], axis=1))

    def bounded_update(j, s, shift, carry):
        l, acc = carry
        p = jnp.exp2(s - shift)
        pv = lax.dot_general(v_block(j), p.astype(BF16), (((0,), (0,)), ((), ())),
                             preferred_element_type=F32)
        return l + jnp.sum(p, axis=0, keepdims=True), acc + pv

    def bounded_pair(j0, carry):
        carry = bounded_update(j0, scores(j0), bound, carry)
        return bounded_update(j0 + 1, scores(j0 + 1), bound, carry)

    def bounded_path():
        init = (jnp.zeros((1, 2 * tq), F32), jnp.zeros((DA_V_DIM, 2 * tq), F32))
        l, acc = lax.fori_loop(0, (nq // 2) * i, lambda t, c: bounded_pair(2 * t, c), init)
        for g in range(nq):
            cols = from_group(g)
            part = bounded_update(nq * i + g, diagonal_scores(g), bound[:, cols],
                                  (l[:, cols], acc[:, cols]))
            l, acc = merge(l, part[0], g), merge(acc, part[1], g)
        return l, acc

    l, acc = lax.cond(jnp.max(bound) <= MAX_SAFE_SCORE_BOUND, bounded_path, online_path)
    o_t = acc * (1.0 / l)

    lam = (jnp.exp(jnp.sum(lq1_ref[...] * lk1_ref[...], axis=-1, keepdims=True))
           - jnp.exp(jnp.sum(lq2_ref[...] * lk2_ref[...], axis=-1, keepdims=True)) + lambda_init)
    o_map1 = jnp.concatenate([o_t[:, 2 * g * tk:(2 * g + 1) * tk] for g in range(nq)], axis=1)
    o_map2 = jnp.concatenate([o_t[:, (2 * g + 1) * tk:(2 * g + 2) * tk] for g in range(nq)], axis=1)
    d = o_map1 - lam * o_map2
    scale = lax.rsqrt(jnp.mean(d * d, axis=0, keepdims=True) + RMS_EPS) * (1.0 - lambda_init)
    o_ref[0] = ((d * scale).T * sub_ref[...]).astype(BF16)


def _diff_attn(q, k, v, q_sq, k_sq, lq1, lk1, lq2, lk2, sub, lambda_init, tq):
    B, L, _ = q.shape
    grid = (B, DA_HEADS, L // tq)
    full = pl.BlockSpec((1, L, DA_V_DIM), lambda b, h, i: (b, 0, h))
    blk = pl.BlockSpec((1, tq, DA_V_DIM), lambda b, h, i: (b, i, h))
    return pl.pallas_call(
        functools.partial(_attn_kernel, tq=tq, lambda_init=lambda_init),
        grid=grid,
        scratch_shapes=[pltpu.VMEM((2 * tq, DA_V_DIM), BF16)],
        in_specs=[blk, full, full,
                  pl.BlockSpec((1, 1, 2, tq), lambda b, h, i: (b, h, 0, i)),
                  pl.BlockSpec((1, 1, 2, L), lambda b, h, i: (b, h, 0, 0)),
                  _const_spec(lq1.shape), _const_spec(lk1.shape),
                  _const_spec(lq2.shape), _const_spec(lk2.shape), _const_spec(sub.shape)],
        out_specs=blk,
        out_shape=jax.ShapeDtypeStruct((B, L, DA_WIDTH), BF16),
        compiler_params=pltpu.CompilerParams(
            dimension_semantics=("arbitrary", "arbitrary", "arbitrary"),
            vmem_limit_bytes=VMEM_LIMIT),
        name="diff_attn",
    )(q, k, v, q_sq, k_sq, lq1, lk1, lq2, lk2, sub)


def _ffn_tile(o_ref, i, nw_ref, wup_ref, cw_ref, cb_ref, wdn_ref, buf_ref, tm):
    hn = _rms(o_ref[0], nw_ref[...]).astype(BF16)
    lead = SUBLANES - (FFN_CONV - 1)

    @pl.when(i == 0)
    def _():
        buf_ref[:, 0:SUBLANES, :] = jnp.zeros((buf_ref.shape[0], SUBLANES, buf_ref.shape[2]), F32)

    starts = [sum(FFN_CHUNKS[:c]) for c in range(len(FFN_CHUNKS))]

    def slot(c, part):
        return buf_ref.at[2 * c + part, :, 0:FFN_CHUNKS[c]]

    def up_project(c):
        for part in range(2):
            col = part * D_FF + starts[c]
            slot(c, part)[SUBLANES:SUBLANES + tm, :] = _dot(hn, wup_ref[:, col:col + FFN_CHUNKS[c]])

    up_project(0)
    for c, width in enumerate(FFN_CHUNKS):
        if c + 1 < len(FFN_CHUNKS):
            up_project(c + 1)
        acts = []
        for part in range(2):
            col = part * D_FF + starts[c]
            taps = [cw_ref[k:k + 1, col:col + width] for k in range(FFN_CONV)]
            acts.append(_causal_conv(slot(c, part), tm, taps, lead) + cb_ref[:, col:col + width])
            slot(c, part)[0:SUBLANES, :] = slot(c, part)[tm:tm + SUBLANES, :]
        act = (_silu(acts[0]) * acts[1]).astype(BF16)
        o_ref[0] += _dot(act, wdn_ref[starts[c]:starts[c] + width, :])


def _out_ffn_kernel(h_ref, ys_ref, ya_ref, wo_ref, nw_ref, wup_ref, cw_ref, cb_ref, wdn_ref,
                    o_ref, buf_ref):
    i = pl.program_id(1)
    tm = h_ref.shape[1]
    o_ref[0] = (h_ref[0] + _dot(ys_ref[0], wo_ref[0:SSD_D_INNER, :])
                + _dot(ya_ref[0], wo_ref[SSD_D_INNER:SSD_D_INNER + DA_WIDTH, :]))
    _ffn_tile(o_ref, i, nw_ref, wup_ref, cw_ref, cb_ref, wdn_ref, buf_ref, tm)


def _out_ffn(h, ys, ya, wo, nw, wup, cw, cb, wdn, layer, tm):
    B, L, D = h.shape
    grid = (B, L // tm)
    row = lambda w: pl.BlockSpec((1, tm, w), lambda b, i: (b, i, 0))
    return pl.pallas_call(
        _out_ffn_kernel, grid=grid,
        in_specs=[row(D), row(SSD_D_INNER), row(DA_WIDTH), _const_spec(wo.shape),
                  _const_spec(nw.shape), _layer_spec(wup.shape, layer), _const_spec(cw.shape),
                  _const_spec(cb.shape), _layer_spec(wdn.shape, layer)],
        out_specs=row(D),
        out_shape=jax.ShapeDtypeStruct((B, L, D), F32),
        scratch_shapes=[pltpu.VMEM((2 * len(FFN_CHUNKS), tm + SUBLANES, max(FFN_CHUNKS)), F32)],
        compiler_params=pltpu.CompilerParams(dimension_semantics=("arbitrary", "arbitrary"),
                                             vmem_limit_bytes=VMEM_LIMIT),
        name="out_ffn",
    )(h, ys, ya, wo, nw, wup, cw, cb, wdn)


def _sc_ffn_kernel(h_ref, mnw_ref, win_ref, scw_ref, wout_ref, nw_ref, wup_ref, cw_ref, cb_ref,
                   wdn_ref, o_ref, mbuf_ref, buf_ref):
    i = pl.program_id(1)
    tm = h_ref.shape[1]
    h = h_ref[0]
    hn = _rms(h, mnw_ref[...]).astype(BF16)

    @pl.when(i == 0)
    def _():
        mbuf_ref[0:SUBLANES, :] = jnp.zeros((SUBLANES, SC_WIDTH), F32)

    cg = _dot(hn, win_ref[:, SC_WIDTH:2 * SC_WIDTH])
    u = _dot(hn, win_ref[:, 2 * SC_WIDTH:3 * SC_WIDTH])
    bg = _dot(hn, win_ref[:, 0:SC_WIDTH])
    mbuf_ref[SUBLANES:SUBLANES + tm, :] = cg * u
    taps = [scw_ref[k:k + 1, :] for k in range(SC_CONV)]
    conv = _causal_conv(mbuf_ref, tm, taps, SUBLANES - (SC_CONV - 1))
    mbuf_ref[0:SUBLANES, :] = mbuf_ref[tm:tm + SUBLANES, :]
    o_ref[0] = h + _dot((bg * conv).astype(BF16), wout_ref[...])
    _ffn_tile(o_ref, i, nw_ref, wup_ref, cw_ref, cb_ref, wdn_ref, buf_ref, tm)


def _sc_ffn(h, mnw, win, scw, wout, nw, wup, cw, cb, wdn, layer, tm):
    B, L, D = h.shape
    grid = (B, L // tm)
    row = pl.BlockSpec((1, tm, D), lambda b, i: (b, i, 0))
    return pl.pallas_call(
        _sc_ffn_kernel, grid=grid,
        in_specs=[row, _const_spec(mnw.shape), _const_spec(win.shape), _const_spec(scw.shape),
                  _const_spec(wout.shape), _const_spec(nw.shape), _layer_spec(wup.shape, layer),
                  _const_spec(cw.shape), _const_spec(cb.shape), _layer_spec(wdn.shape, layer)],
        out_specs=row,
        out_shape=jax.ShapeDtypeStruct((B, L, D), F32),
        scratch_shapes=[pltpu.VMEM((tm + SUBLANES, SC_WIDTH), F32),
                        pltpu.VMEM((2 * len(FFN_CHUNKS), tm + SUBLANES, max(FFN_CHUNKS)), F32)],
        compiler_params=pltpu.CompilerParams(dimension_semantics=("arbitrary", "arbitrary"),
                                             vmem_limit_bytes=VMEM_LIMIT),
        name="sc_ffn",
    )(h, mnw, win, scw, wout, nw, wup, cw, cb, wdn)


def _rope_table(seq):
    inv = 1.0 / (ROPE_THETA ** (jnp.arange(0, DA_HEAD_DIM, 2, dtype=F32) / DA_HEAD_DIM))
    ang = jnp.arange(seq, dtype=F32)[:, None] * inv[None, :]
    ang = jnp.concatenate([ang, ang], axis=-1)
    cos, sin = jnp.cos(ang), jnp.sin(ang)
    first = jnp.arange(DA_HEAD_DIM) < DA_HEAD_DIM // 2
    sin_a = jnp.where(first, -sin, 0.0)
    sin_b = jnp.where(first, 0.0, sin)
    rep = LANES // DA_HEAD_DIM
    return jnp.concatenate([jnp.tile(cos, (1, rep)), jnp.tile(sin_a, (1, rep)),
                            jnp.tile(sin_b, (1, rep))], axis=-1)


def _row(v):
    return v.reshape(1, -1).astype(F32)


def _pad_lanes(v):
    return jnp.pad(v.reshape(1, -1).astype(F32), ((0, 0), (0, LANES - v.shape[-1])))


def kernel(x, mix_norm, ffn_norm, hy_w_in, hy_conv_w, hy_conv_b, hy_dt_bias, hy_a_log, hy_d_skip,
           hy_ssd_norm, hy_q_norm, hy_k_norm, hy_lambda_q1, hy_lambda_k1, hy_lambda_q2,
           hy_lambda_k2, hy_subln, hy_w_out, sc_w_in, sc_conv_w, sc_w_out, ffn_w_up, ffn_conv_w,
           ffn_conv_b, ffn_w_down):
    B, L, D = x.shape
    tm = min(512, L)
    tq = min(2048, L)

    w = hy_w_in[0]
    o_x = SSD_D_INNER
    o_dt = o_x + SSD_CONV_DIM
    o_q = o_dt + SSD_HEADS
    o_k = o_q + DA_WIDTH
    o_v = o_k + DA_WIDTH
    w_in = jnp.concatenate(
        [w[:, 0:o_dt].astype(BF16),
         jnp.pad(w[:, o_dt:o_q], ((0, 0), (0, LANES - SSD_HEADS))).astype(BF16),
         w[:, o_q:o_v + DA_WIDTH].astype(BF16)], axis=1)
    head_of = jnp.arange(DA_WIDTH) // DA_HEAD_DIM
    g = (head_of[:, None] == jnp.arange(LANES)[None, :]).astype(F32)
    gt = jnp.concatenate([g.T, g.T], axis=0).astype(BF16)
    q_gain = jnp.tile(hy_q_norm[0], DA_WIDTH // DA_HEAD_DIM).astype(F32)
    k_gain = jnp.tile(hy_k_norm[0], DA_WIDTH // DA_HEAD_DIM).astype(F32)

    def head_sum_selector(gain):
        weighted = jnp.roll(g * (gain * gain)[:, None], HEADS_QK, axis=1)
        return (g + weighted).astype(BF16)

    rope = _rope_table(L)
    z, xc, dtf, q, k, v, q_sq, k_sq = _even_in(
        x, _row(mix_norm[0]), w_in, hy_conv_w[0].astype(F32),
        _row(hy_conv_b[0]), _pad_lanes(hy_dt_bias[0]), _row(q_gain), _row(k_gain), rope,
        head_sum_selector(q_gain), head_sum_selector(k_gain), gt, tm)
    y_ssd = _ssd(xc, dtf, z, _pad_lanes(hy_a_log[0]),
                 _row(jnp.repeat(hy_d_skip[0], SSD_HEAD_DIM)), _row(hy_ssd_norm[0]), tm)
    lambda_init = 0.8 - 0.6 * math.exp(-0.3 * 0)
    y_att = _diff_attn(q, k, v, q_sq, k_sq,
                       _row(hy_lambda_q1[0]), _row(hy_lambda_k1[0]),
                       _row(hy_lambda_q2[0]), _row(hy_lambda_k2[0]), _row(hy_subln[0]),
                       lambda_init, tq)
    w_up = ffn_w_up.astype(BF16)
    w_down = ffn_w_down.astype(BF16)
    h = _out_ffn(x, y_ssd, y_att, hy_w_out[0].astype(BF16), _row(ffn_norm[0]),
                 w_up, ffn_conv_w[0].astype(F32), _row(ffn_conv_b[0]), w_down, 0, tm)

    h = _sc_ffn(h, _row(mix_norm[1]), sc_w_in[0].astype(BF16), sc_conv_w[0].astype(F32),
                sc_w_out[0].astype(BF16), _row(ffn_norm[1]), w_up,
                ffn_conv_w[1].astype(F32), _row(ffn_conv_b[1]), w_down, 1, tm)
    return h
```

```python
import functools
import math

import jax
import jax.numpy as jnp
from jax import lax
from jax.experimental import pallas as pl
from jax.experimental.pallas import tpu as pltpu

F32 = jnp.float32
BF16 = jnp.bfloat16

RMS_EPS = 1e-6
ROPE_THETA = 10000.0
D_MODEL = 1024
SSD_HEAD_DIM = 64
SSD_HEADS = 16
SSD_GROUPS = 2
SSD_D_STATE = 128
SSD_D_INNER = 1024
SSD_CONV = 4
SSD_CHUNK = 128
SSD_CONV_DIM = SSD_D_INNER + 2 * SSD_GROUPS * SSD_D_STATE
DA_HEADS = 8
DA_HEAD_DIM = 64
HEADS_QK = 2 * DA_HEADS
EVEN_W_X = 1024
EVEN_W_DT = EVEN_W_X + 1536
EVEN_W_Q = EVEN_W_DT + 128
EVEN_W_K = EVEN_W_Q + 1024
EVEN_W_V = EVEN_W_K + 1024
EVEN_W_END = EVEN_W_V + 1024
DA_V_DIM = 128
DA_WIDTH = 1024
SC_WIDTH = 1024
SC_CONV = 3
D_FF = 2816
FFN_CONV = 3
FFN_CHUNKS = (768, 768, 768, 512)
LANES = 128
SUBLANES = 8
VMEM_LIMIT = 56 * 1024 * 1024
LOG2E = 1.4426950408889634
ATTN_KEY_BLOCK = 512
MAX_SAFE_SCORE_BOUND = 60.0


def _dot(a, b):
    return jnp.dot(a, b, preferred_element_type=F32)


def _split3(x):
    hi = x.astype(BF16)
    r1 = x - hi.astype(F32)
    mid = r1.astype(BF16)
    lo = (r1 - mid.astype(F32)).astype(BF16)
    return hi, mid, lo


def _dot_sel_rhs(x, sel):
    hi, mid, lo = _split3(x)
    return _dot(hi, sel) + _dot(mid, sel) + _dot(lo, sel)


def _dot_sel_rhs16(x, sel2):
    hi = x.astype(BF16)
    mid = (x - hi.astype(F32)).astype(BF16)
    return _dot(jnp.concatenate([hi, mid], axis=1), sel2)


def _dot_sel_lhs(sel, x):
    hi, mid, lo = _split3(x)
    return _dot(sel, hi) + _dot(sel, mid) + _dot(sel, lo)


def _rms(x, w):
    return x * lax.rsqrt(jnp.mean(x * x, axis=-1, keepdims=True) + RMS_EPS) * w


def _silu(x):
    return x * (1.0 / (1.0 + jnp.exp(-x)))


def _softplus(x):
    return jnp.maximum(x, 0.0) + jnp.log1p(jnp.exp(-jnp.abs(x)))


def _const_spec(shape):
    nd = len(shape)
    return pl.BlockSpec(shape, lambda *_: (0,) * nd, pipeline_mode=pl.Buffered(1))


def _layer_spec(stacked_shape, layer):
    _, rows, cols = stacked_shape
    return pl.BlockSpec((None, rows, cols), lambda *_: (layer, 0, 0), pipeline_mode=pl.Buffered(1))


def _causal_conv(buf_ref, tm, taps, first_row):
    xe = buf_ref[...]
    acc = None
    for k, w in enumerate(taps):
        back = SUBLANES - (first_row + k)
        shifted = xe if back == 0 else pltpu.roll(xe, back, 0)
        term = shifted[SUBLANES:SUBLANES + tm] * w
        acc = term if acc is None else acc + term
    return acc


def _even_in_kernel(h_ref, nw_ref, w_ref, cw_ref, cb_ref, dtb_ref, qn_ref, kn_ref, rope_ref,
                    gq_ref, gk_ref, gt_ref,
                    z_ref, xc_ref, dt_ref, q_ref, k_ref, v_ref, qsq_ref, ksq_ref, xs_ref):
    i = pl.program_id(1)
    tm = h_ref.shape[1]
    hn = _rms(h_ref[0], nw_ref[...]).astype(BF16)

    @pl.when(i == 0)
    def _():
        xs_ref[0:SUBLANES, :] = jnp.zeros((SUBLANES, SSD_CONV_DIM), F32)

    xs_ref[SUBLANES:SUBLANES + tm, :] = _dot(hn, w_ref[:, EVEN_W_X:EVEN_W_DT])
    q_raw = _dot(hn, w_ref[:, EVEN_W_Q:EVEN_W_K])

    taps = [cw_ref[k:k + 1, :] for k in range(SSD_CONV)]
    conv = _causal_conv(xs_ref, tm, taps, SUBLANES - (SSD_CONV - 1)) + cb_ref[...]
    xc_ref[0] = _silu(conv).astype(BF16)
    xs_ref[0:SUBLANES, :] = xs_ref[tm:tm + SUBLANES, :]

    k_raw = _dot(hn, w_ref[:, EVEN_W_K:EVEN_W_V])

    cos = jnp.tile(rope_ref[:, 0:LANES], (1, DA_WIDTH // LANES))
    sin_a = jnp.tile(rope_ref[:, LANES:2 * LANES], (1, DA_WIDTH // LANES))
    sin_b = jnp.tile(rope_ref[:, 2 * LANES:3 * LANES], (1, DA_WIDTH // LANES))
    half = DA_HEAD_DIM // 2

    def qk_epilogue(x, n_ref, g_ref, o_ref, sq_ref, scale):
        ss = _dot((x * x).astype(BF16), g_ref[...])
        r = lax.rsqrt(ss * (1.0 / DA_HEAD_DIM) + RMS_EPS) * scale
        xn = x * _dot_sel_rhs16(r, gt_ref[...]) * n_ref[...]
        rot = (pltpu.roll(xn, DA_WIDTH - half, 1) * sin_a + pltpu.roll(xn, half, 1) * sin_b)
        o_ref[0] = (xn * cos + rot).astype(BF16)
        nsq_t = (r * r * pltpu.roll(ss, LANES - HEADS_QK, 1)).T
        for hd in range(DA_HEADS):
            sq_ref[0, hd] = nsq_t[2 * hd:2 * hd + 2, :]

    qk_epilogue(q_raw, qn_ref, gq_ref, q_ref, qsq_ref, LOG2E * DA_HEAD_DIM ** -0.5)
    z_ref[0] = _dot(hn, w_ref[:, 0:EVEN_W_X]).astype(BF16)
    qk_epilogue(k_raw, kn_ref, gk_ref, k_ref, ksq_ref, 1.0)
    v_ref[0] = _dot(hn, w_ref[:, EVEN_W_V:EVEN_W_END]).astype(BF16)
    dt_ref[0] = _softplus(_dot(hn, w_ref[:, EVEN_W_DT:EVEN_W_Q]) + dtb_ref[...])


def _even_in(h, nw, w_in, cw, cb, dtb, qn, kn, rope, gq, gk, gt, tm):
    B, L, D = h.shape
    grid = (B, L // tm)
    row = lambda w: pl.BlockSpec((1, tm, w), lambda b, i: (b, i, 0))
    out_shape = (jax.ShapeDtypeStruct((B, L, SSD_D_INNER), BF16),
                 jax.ShapeDtypeStruct((B, L, SSD_CONV_DIM), BF16),
                 jax.ShapeDtypeStruct((B, L, LANES), F32),
                 jax.ShapeDtypeStruct((B, L, DA_WIDTH), BF16),
                 jax.ShapeDtypeStruct((B, L, DA_WIDTH), BF16),
                 jax.ShapeDtypeStruct((B, L, DA_WIDTH), BF16),
                 jax.ShapeDtypeStruct((B, DA_HEADS, 2, L), F32),
                 jax.ShapeDtypeStruct((B, DA_HEADS, 2, L), F32))
    head_rows = pl.BlockSpec((1, DA_HEADS, 2, tm), lambda b, i: (b, 0, 0, i))
    in_specs = [row(D), _const_spec(nw.shape), _const_spec(w_in.shape),
                _const_spec(cw.shape), _const_spec(cb.shape),
                _const_spec(dtb.shape), _const_spec(qn.shape), _const_spec(kn.shape),
                pl.BlockSpec((tm, 3 * LANES), lambda b, i: (i, 0)),
                _const_spec(gq.shape), _const_spec(gk.shape), _const_spec(gt.shape)]
    out_specs = (row(SSD_D_INNER), row(SSD_CONV_DIM), row(LANES), row(DA_WIDTH), row(DA_WIDTH),
                 row(DA_WIDTH), head_rows, head_rows)
    return pl.pallas_call(
        _even_in_kernel, grid=grid, in_specs=in_specs, out_specs=out_specs, out_shape=out_shape,
        scratch_shapes=[pltpu.VMEM((tm + SUBLANES, SSD_CONV_DIM), F32)],
        compiler_params=pltpu.CompilerParams(dimension_semantics=("arbitrary", "arbitrary"),
                                             vmem_limit_bytes=VMEM_LIMIT),
        name="even_in",
    )(h, nw, w_in, cw, cb, dtb, qn, kn, rope, gq, gk, gt)


def _ssd_kernel(xc_ref, dt_ref, z_ref, alog_ref, dskip_ref, nw_ref, tril_ref, expand_ref,
                y_ref, state_ref):
    T = SSD_CHUNK
    GN = SSD_GROUPS * SSD_D_STATE
    GW = SSD_D_INNER // SSD_GROUPS
    heads_per_group = SSD_HEADS // SSD_GROUPS

    @pl.when(pl.program_id(1) == 0)
    def _():
        state_ref[...] = jnp.zeros(state_ref.shape, F32)

    a = -jnp.exp(alog_ref[...])
    r_i = lax.broadcasted_iota(jnp.int32, (T, T), 0)
    c_i = lax.broadcasted_iota(jnp.int32, (T, T), 1)
    tril = r_i >= c_i
    lane = lax.broadcasted_iota(jnp.int32, (T, LANES), 1)

    for ci in range(xc_ref.shape[1] // T):
        rows = slice(ci * T, (ci + 1) * T)
        x = xc_ref[0, rows, 0:SSD_D_INNER].astype(F32)
        bm = xc_ref[0, rows, SSD_D_INNER:SSD_D_INNER + GN]
        cm = xc_ref[0, rows, SSD_D_INNER + GN:SSD_D_INNER + 2 * GN]
        dt = dt_ref[0, rows, :]
        a_cs = _dot_sel_lhs(tril_ref[...], dt * a)
        a_log2 = a_cs * LOG2E
        a_log2_t = a_log2.T
        a_last = a_cs[T - 1:T, :]

        stacked = jnp.concatenate(
            [dt, jnp.exp(a_cs), jnp.exp(a_last - a_cs),
             jnp.broadcast_to(jnp.exp(a_last), (SUBLANES, LANES))], axis=0)
        ex = _dot_sel_rhs16(stacked, expand_ref[...])
        dt_e, expa_e, decay_e = ex[0:T], ex[T:2 * T], ex[2 * T:3 * T]
        chunk_decay = ex[3 * T:3 * T + 1]
        xdt = x * dt_e

        xdt_b = xdt.astype(BF16)
        y_parts = []
        for g in range(SSD_GROUPS):
            cg = cm[:, g * SSD_D_STATE:(g + 1) * SSD_D_STATE]
            bg = bm[:, g * SSD_D_STATE:(g + 1) * SSD_D_STATE]
            cb = lax.dot_general(cg, bg, (((1,), (1,)), ((), ())), preferred_element_type=F32)
            for hp in range(heads_per_group // 2):
                h0 = g * heads_per_group + 2 * hp
                ms = []
                for h in (h0, h0 + 1):
                    seg = a_log2[:, h:h + 1] - a_log2_t[h:h + 1, :]
                    ms.append((cb * jnp.exp2(jnp.where(tril, seg, -jnp.inf))).astype(BF16))
                lhs = jnp.concatenate(ms, axis=1)
                xp = xdt_b[:, h0 * SSD_HEAD_DIM:(h0 + 2) * SSD_HEAD_DIM]
                rhs = jnp.concatenate([jnp.where(lane < SSD_HEAD_DIM, xp, 0),
                                       jnp.where(lane >= SSD_HEAD_DIM, xp, 0)], axis=0)
                y_parts.append(_dot(lhs, rhs))
        y_diag = jnp.concatenate(y_parts, axis=1)

        xdec = (xdt * decay_e).astype(BF16)
        y_off_parts = []
        for g in range(SSD_GROUPS):
            cg = cm[:, g * SSD_D_STATE:(g + 1) * SSD_D_STATE]
            y_off_parts.append(_dot(cg, state_ref[g].astype(BF16)))
            bg_t = bm[:, g * SSD_D_STATE:(g + 1) * SSD_D_STATE].astype(F32).T.astype(BF16)
            upd = _dot(bg_t, xdec[:, g * GW:(g + 1) * GW])
            state_ref[g] = state_ref[g] * chunk_decay[:, g * GW:(g + 1) * GW] + upd
        y_off = jnp.concatenate(y_off_parts, axis=1) * expa_e

        y = y_diag + y_off + dskip_ref[...] * x
        y = y * _silu(z_ref[0, rows, :].astype(F32))
        outs = []
        for g in range(SSD_GROUPS):
            yg = y[:, g * GW:(g + 1) * GW]
            outs.append(yg * lax.rsqrt(jnp.mean(yg * yg, axis=-1, keepdims=True) + RMS_EPS))
        y_ref[0, rows, :] = (jnp.concatenate(outs, axis=1) * nw_ref[...]).astype(BF16)


def _ssd(xc, dtf, z, alog, dskip, nw, ts):
    B, L, _ = xc.shape
    T = SSD_CHUNK
    grid = (B, L // ts)
    row = lambda w: pl.BlockSpec((1, ts, w), lambda b, c: (b, c, 0))
    tril = (jnp.arange(T)[:, None] >= jnp.arange(T)[None, :]).astype(BF16)
    expand = (jnp.arange(SSD_D_INNER)[None, :] // SSD_HEAD_DIM
              == jnp.arange(2 * LANES)[:, None] % LANES).astype(BF16)
    return pl.pallas_call(
        _ssd_kernel, grid=grid,
        in_specs=[row(SSD_CONV_DIM), row(LANES), row(SSD_D_INNER),
                  _const_spec(alog.shape), _const_spec(dskip.shape), _const_spec(nw.shape),
                  _const_spec(tril.shape), _const_spec(expand.shape)],
        out_specs=row(SSD_D_INNER),
        out_shape=jax.ShapeDtypeStruct((B, L, SSD_D_INNER), BF16),
        scratch_shapes=[pltpu.VMEM((SSD_GROUPS, SSD_D_STATE, SSD_D_INNER // SSD_GROUPS), F32)],
        compiler_params=pltpu.CompilerParams(dimension_semantics=("arbitrary", "arbitrary"),
                                             vmem_limit_bytes=VMEM_LIMIT),
        name="ssd",
    )(xc, dtf, z, alog, dskip, nw, tril, expand)


def _attn_kernel(q_ref, k_ref, v_ref, qsq_ref, ksq_ref, lq1_ref, lk1_ref, lq2_ref, lk2_ref, sub_ref,
                 o_ref, q2_ref, *, tq, lambda_init):
    i = pl.program_id(2)
    tk = ATTN_KEY_BLOCK
    nq = tq // tk
    q = q_ref[0]
    lane = lax.broadcasted_iota(jnp.int32, q.shape, 1)
    zero = jnp.zeros_like(q)
    q_map1 = jnp.where(lane < DA_HEAD_DIM, q, zero)
    q_map2 = jnp.where(lane >= DA_HEAD_DIM, q, zero)
    for g in range(nq):
        q2_ref[2 * g * tk:(2 * g + 1) * tk, :] = q_map1[g * tk:(g + 1) * tk]
        q2_ref[(2 * g + 1) * tk:(2 * g + 2) * tk, :] = q_map2[g * tk:(g + 1) * tk]

    def from_group(g):
        return slice(2 * g * tk, 2 * tq)

    def scores(j, first_group=0):
        kb = k_ref[0, pl.ds(pl.multiple_of(j * tk, tk), tk), :]
        return lax.dot_general(kb, q2_ref[from_group(first_group), :], (((1,), (1,)), ((), ())),
                               preferred_element_type=F32)

    def causal(s):
        kpos = lax.broadcasted_iota(jnp.int32, s.shape, 0)
        col = lax.broadcasted_iota(jnp.int32, s.shape, 1)
        qpos = jnp.where(col >= tk, col - tk, col)
        return jnp.where(kpos <= qpos, s, -jnp.inf)

    def diagonal_scores(g):
        s = scores(nq * i + g, first_group=g)
        if g + 1 == nq:
            return causal(s)
        return jnp.concatenate([causal(s[:, 0:2 * tk]), s[:, 2 * tk:]], axis=1)

    def merge(full, part, g):
        return part if g == 0 else jnp.concatenate([full[:, 0:2 * g * tk], part], axis=1)

    def v_block(j):
        return v_ref[0, pl.ds(pl.multiple_of(j * tk, tk), tk), :]

    def update(j, s, carry):
        m, l, acc = carry
        m_new = jnp.maximum(m, jnp.max(s, axis=0, keepdims=True))
        p = jnp.exp2(s - m_new)
        alpha = jnp.exp2(m - m_new)
        l_new = alpha * l + jnp.sum(p, axis=0, keepdims=True)
        pv = lax.dot_general(v_block(j), p.astype(BF16), (((0,), (0,)), ((), ())),
                             preferred_element_type=F32)
        return m_new, l_new, alpha * acc + pv

    def pair(j0, carry):
        s_a = scores(j0)
        s_b = scores(j0 + 1)
        return update(j0 + 1, s_b, update(j0, s_a, carry))

    def online_path():
        init = (jnp.full((1, 2 * tq), -jnp.inf, F32), jnp.zeros((1, 2 * tq), F32),
                jnp.zeros((DA_V_DIM, 2 * tq), F32))
        m, l, acc = lax.fori_loop(0, (nq // 2) * i, lambda t, c: pair(2 * t, c), init)
        for g in range(nq):
            cols = from_group(g)
            part = update(nq * i + g, diagonal_scores(g), (m[:, cols], l[:, cols], acc[:, cols]))
            m, l, acc = merge(m, part[0], g), merge(l, part[1], g), merge(acc, part[2], g)
        return l, acc

    qsq = qsq_ref[0, 0]
    kmax = jnp.max(ksq_ref[0, 0], axis=1, keepdims=True)
    bound = jnp.sqrt(jnp.concatenate(
        [qsq[mp:mp + 1, g * tk:(g + 1) * tk] * kmax[mp:mp + 1]
         for g in range(nq) for mp in range(2)], axis=1))

    def bounded_update(j, s, shift, carry):
        l, acc = carry
        p = jnp.exp2(s - shift)
        pv = lax.dot_general(v_block(j), p.astype(BF16), (((0,), (0,)), ((), ())),
                             preferred_element_type=F32)
        return l + jnp.sum(p, axis=0, keepdims=True), acc + pv

    def bounded_pair(j0, carry):
        carry = bounded_update(j0, scores(j0), bound, carry)
        return bounded_update(j0 + 1, scores(j0 + 1), bound, carry)

    def bounded_path():
        init = (jnp.zeros((1, 2 * tq), F32), jnp.zeros((DA_V_DIM, 2 * tq), F32))
        l, acc = lax.fori_loop(0, (nq // 2) * i, lambda t, c: bounded_pair(2 * t, c), init)
        for g in range(nq):
            cols = from_group(g)
            part = bounded_update(nq * i + g, diagonal_scores(g), bound[:, cols],
                                  (l[:, cols], acc[:, cols]))
            l, acc = merge(l, part[0], g), merge(acc, part[1], g)
        return l, acc

    l, acc = lax.cond(jnp.max(bound) <= MAX_SAFE_SCORE_BOUND, bounded_path, online_path)
    o_t = acc * (1.0 / l)

    lam = (jnp.exp(jnp.sum(lq1_ref[...] * lk1_ref[...], axis=-1, keepdims=True))
           - jnp.exp(jnp.sum(lq2_ref[...] * lk2_ref[...], axis=-1, keepdims=True)) + lambda_init)
    o_map1 = jnp.concatenate([o_t[:, 2 * g * tk:(2 * g + 1) * tk] for g in range(nq)], axis=1)
    o_map2 = jnp.concatenate([o_t[:, (2 * g + 1) * tk:(2 * g + 2) * tk] for g in range(nq)], axis=1)
    d = o_map1 - lam * o_map2
    scale = lax.rsqrt(jnp.mean(d * d, axis=0, keepdims=True) + RMS_EPS) * (1.0 - lambda_init)
    o_ref[0] = ((d * scale).T * sub_ref[...]).astype(BF16)


def _diff_attn(q, k, v, q_sq, k_sq, lq1, lk1, lq2, lk2, sub, lambda_init, tq):
    B, L, _ = q.shape
    grid = (B, DA_HEADS, L // tq)
    full = pl.BlockSpec((1, L, DA_V_DIM), lambda b, h, i: (b, 0, h))
    blk = pl.BlockSpec((1, tq, DA_V_DIM), lambda b, h, i: (b, i, h))
    return pl.pallas_call(
        functools.partial(_attn_kernel, tq=tq, lambda_init=lambda_init),
        grid=grid,
        scratch_shapes=[pltpu.VMEM((2 * tq, DA_V_DIM), BF16)],
        in_specs=[blk, full, full,
                  pl.BlockSpec((1, 1, 2, tq), lambda b, h, i: (b, h, 0, i)),
                  pl.BlockSpec((1, 1, 2, L), lambda b, h, i: (b, h, 0, 0)),
                  _const_spec(lq1.shape), _const_spec(lk1.shape),
                  _const_spec(lq2.shape), _const_spec(lk2.shape), _const_spec(sub.shape)],
        out_specs=blk,
        out_shape=jax.ShapeDtypeStruct((B, L, DA_WIDTH), BF16),
        compiler_params=pltpu.CompilerParams(
            dimension_semantics=("arbitrary", "arbitrary", "arbitrary"),
            vmem_limit_bytes=VMEM_LIMIT),
        name="diff_attn",
    )(q, k, v, q_sq, k_sq, lq1, lk1, lq2, lk2, sub)


def _ffn_tile(o_ref, i, nw_ref, wup_ref, cw_ref, cb_ref, wdn_ref, buf_ref, tm):
    hn = _rms(o_ref[0], nw_ref[...]).astype(BF16)
    lead = SUBLANES - (FFN_CONV - 1)

    @pl.when(i == 0)
    def _():
        buf_ref[:, 0:SUBLANES, :] = jnp.zeros((buf_ref.shape[0], SUBLANES, buf_ref.shape[2]), F32)

    starts = [sum(FFN_CHUNKS[:c]) for c in range(len(FFN_CHUNKS))]

    def slot(c, part):
        return buf_ref.at[2 * c + part, :, 0:FFN_CHUNKS[c]]

    def up_project(c):
        for part in range(2):
            col = part * D_FF + starts[c]
            slot(c, part)[SUBLANES:SUBLANES + tm, :] = _dot(hn, wup_ref[:, col:col + FFN_CHUNKS[c]])

    up_project(0)
    for c, width in enumerate(FFN_CHUNKS):
        if c + 1 < len(FFN_CHUNKS):
            up_project(c + 1)
        acts = []
        for part in range(2):
            col = part * D_FF + starts[c]
            taps = [cw_ref[k:k + 1, col:col + width] for k in range(FFN_CONV)]
            acts.append(_causal_conv(slot(c, part), tm, taps, lead) + cb_ref[:, col:col + width])
            slot(c, part)[0:SUBLANES, :] = slot(c, part)[tm:tm + SUBLANES, :]
        act = (_silu(acts[0]) * acts[1]).astype(BF16)
        o_ref[0] += _dot(act, wdn_ref[starts[c]:starts[c] + width, :])


def _out_ffn_kernel(h_ref, ys_ref, ya_ref, wo_ref, nw_ref, wup_ref, cw_ref, cb_ref, wdn_ref,
                    o_ref, buf_ref):
    i = pl.program_id(1)
    tm = h_ref.shape[1]
    o_ref[0] = (h_ref[0] + _dot(ys_ref[0], wo_ref[0:SSD_D_INNER, :])
                + _dot(ya_ref[0], wo_ref[SSD_D_INNER:SSD_D_INNER + DA_WIDTH, :]))
    _ffn_tile(o_ref, i, nw_ref, wup_ref, cw_ref, cb_ref, wdn_ref, buf_ref, tm)


def _out_ffn(h, ys, ya, wo, nw, wup, cw, cb, wdn, layer, tm):
    B, L, D = h.shape
    grid = (B, L // tm)
    row = lambda w: pl.BlockSpec((1, tm, w), lambda b, i: (b, i, 0))
    return pl.pallas_call(
        _out_ffn_kernel, grid=grid,
        in_specs=[row(D), row(SSD_D_INNER), row(DA_WIDTH), _const_spec(wo.shape),
                  _const_spec(nw.shape), _layer_spec(wup.shape, layer), _const_spec(cw.shape),
                  _const_spec(cb.shape), _layer_spec(wdn.shape, layer)],
        out_specs=row(D),
        out_shape=jax.ShapeDtypeStruct((B, L, D), F32),
        scratch_shapes=[pltpu.VMEM((2 * len(FFN_CHUNKS), tm + SUBLANES, max(FFN_CHUNKS)), F32)],
        compiler_params=pltpu.CompilerParams(dimension_semantics=("arbitrary", "arbitrary"),
                                             vmem_limit_bytes=VMEM_LIMIT),
        name="out_ffn",
    )(h, ys, ya, wo, nw, wup, cw, cb, wdn)


def _sc_ffn_kernel(h_ref, mnw_ref, win_ref, scw_ref, wout_ref, nw_ref, wup_ref, cw_ref, cb_ref,
                   wdn_ref, o_ref, mbuf_ref, buf_ref):
    i = pl.program_id(1)
    tm = h_ref.shape[1]
    h = h_ref[0]
    hn = _rms(h, mnw_ref[...]).astype(BF16)

    @pl.when(i == 0)
    def _():
        mbuf_ref[0:SUBLANES, :] = jnp.zeros((SUBLANES, SC_WIDTH), F32)

    cg = _dot(hn, win_ref[:, SC_WIDTH:2 * SC_WIDTH])
    u = _dot(hn, win_ref[:, 2 * SC_WIDTH:3 * SC_WIDTH])
    bg = _dot(hn, win_ref[:, 0:SC_WIDTH])
    mbuf_ref[SUBLANES:SUBLANES + tm, :] = cg * u
    taps = [scw_ref[k:k + 1, :] for k in range(SC_CONV)]
    conv = _causal_conv(mbuf_ref, tm, taps, SUBLANES - (SC_CONV - 1))
    mbuf_ref[0:SUBLANES, :] = mbuf_ref[tm:tm + SUBLANES, :]
    o_ref[0] = h + _dot((bg * conv).astype(BF16), wout_ref[...])
    _ffn_tile(o_ref, i, nw_ref, wup_ref, cw_ref, cb_ref, wdn_ref, buf_ref, tm)


def _sc_ffn(h, mnw, win, scw, wout, nw, wup, cw, cb, wdn, layer, tm):
    B, L, D = h.shape
    grid = (B, L // tm)
    row = pl.BlockSpec((1, tm, D), lambda b, i: (b, i, 0))
    return pl.pallas_call(
        _sc_ffn_kernel, grid=grid,
        in_specs=[row, _const_spec(mnw.shape), _const_spec(win.shape), _const_spec(scw.shape),
                  _const_spec(wout.shape), _const_spec(nw.shape), _layer_spec(wup.shape, layer),
                  _const_spec(cw.shape), _const_spec(cb.shape), _layer_spec(wdn.shape, layer)],
        out_specs=row,
        out_shape=jax.ShapeDtypeStruct((B, L, D), F32),
        scratch_shapes=[pltpu.VMEM((tm + SUBLANES, SC_WIDTH), F32),
                        pltpu.VMEM((2 * len(FFN_CHUNKS), tm + SUBLANES, max(FFN_CHUNKS)), F32)],
        compiler_params=pltpu.CompilerParams(dimension_semantics=("arbitrary", "arbitrary"),
                                             vmem_limit_bytes=VMEM_LIMIT),
        name="sc_ffn",
    )(h, mnw, win, scw, wout, nw, wup, cw, cb, wdn)


def _rope_table(seq):
    inv = 1.0 / (ROPE_THETA ** (jnp.arange(0, DA_HEAD_DIM, 2, dtype=F32) / DA_HEAD_DIM))
    ang = jnp.arange(seq, dtype=F32)[:, None] * inv[None, :]
    ang = jnp.concatenate([ang, ang], axis=-1)
    cos, sin = jnp.cos(ang), jnp.sin(ang)
    first = jnp.arange(DA_HEAD_DIM) < DA_HEAD_DIM // 2
    sin_a = jnp.where(first, -sin, 0.0)
    sin_b = jnp.where(first, 0.0, sin)
    rep = LANES // DA_HEAD_DIM
    return jnp.concatenate([jnp.tile(cos, (1, rep)), jnp.tile(sin_a, (1, rep)),
                            jnp.tile(sin_b, (1, rep))], axis=-1)


def _row(v):
    return v.reshape(1, -1).astype(F32)


def _pad_lanes(v):
    return jnp.pad(v.reshape(1, -1).astype(F32), ((0, 0), (0, LANES - v.shape[-1])))


def kernel(x, mix_norm, ffn_norm, hy_w_in, hy_conv_w, hy_conv_b, hy_dt_bias, hy_a_log, hy_d_skip,
           hy_ssd_norm, hy_q_norm, hy_k_norm, hy_lambda_q1, hy_lambda_k1, hy_lambda_q2,
           hy_lambda_k2, hy_subln, hy_w_out, sc_w_in, sc_conv_w, sc_w_out, ffn_w_up, ffn_conv_w,
           ffn_conv_b, ffn_w_down):
    B, L, D = x.shape
    tm = min(512, L)
    tq = min(4096, L)

    w = hy_w_in[0]
    o_x = SSD_D_INNER
    o_dt = o_x + SSD_CONV_DIM
    o_q = o_dt + SSD_HEADS
    o_k = o_q + DA_WIDTH
    o_v = o_k + DA_WIDTH
    w_in = jnp.concatenate(
        [w[:, 0:o_dt].astype(BF16),
         jnp.pad(w[:, o_dt:o_q], ((0, 0), (0, LANES - SSD_HEADS))).astype(BF16),
         w[:, o_q:o_v + DA_WIDTH].astype(BF16)], axis=1)
    head_of = jnp.arange(DA_WIDTH) // DA_HEAD_DIM
    g = (head_of[:, None] == jnp.arange(LANES)[None, :]).astype(F32)
    gt = jnp.concatenate([g.T, g.T], axis=0).astype(BF16)
    q_gain = jnp.tile(hy_q_norm[0], DA_WIDTH // DA_HEAD_DIM).astype(F32)
    k_gain = jnp.tile(hy_k_norm[0], DA_WIDTH // DA_HEAD_DIM).astype(F32)

    def head_sum_selector(gain):
        weighted = jnp.roll(g * (gain * gain)[:, None], HEADS_QK, axis=1)
        return (g + weighted).astype(BF16)

    rope = _rope_table(L)
    z, xc, dtf, q, k, v, q_sq, k_sq = _even_in(
        x, _row(mix_norm[0]), w_in, hy_conv_w[0].astype(F32),
        _row(hy_conv_b[0]), _pad_lanes(hy_dt_bias[0]), _row(q_gain), _row(k_gain), rope,
        head_sum_selector(q_gain), head_sum_selector(k_gain), gt, tm)
    y_ssd = _ssd(xc, dtf, z, _pad_lanes(hy_a_log[0]),
                 _row(jnp.repeat(hy_d_skip[0], SSD_HEAD_DIM)), _row(hy_ssd_norm[0]), tm)
    lambda_init = 0.8 - 0.6 * math.exp(-0.3 * 0)
    y_att = _diff_attn(q, k, v, q_sq, k_sq,
                       _row(hy_lambda_q1[0]), _row(hy_lambda_k1[0]),
                       _row(hy_lambda_q2[0]), _row(hy_lambda_k2[0]), _row(hy_subln[0]),
                       lambda_init, tq)
    w_up = ffn_w_up.astype(BF16)
    w_down = ffn_w_down.astype(BF16)
    h = _out_ffn(x, y_ssd, y_att, hy_w_out[0].astype(BF16), _row(ffn_norm[0]),
                 w_up, ffn_conv_w[0].astype(F32), _row(ffn_conv_b[0]), w_down, 0, tm)

    h = _sc_ffn(h, _row(mix_norm[1]), sc_w_in[0].astype(BF16), sc_conv_w[0].astype(F32),
                sc_w_out[0].astype(BF16), _row(ffn_norm[1]), w_up,
                ffn_conv_w[1].astype(F32), _row(ffn_conv_b[1]), w_down, 1, tm)
    return h
```

```python
import functools
import math

import jax
import jax.numpy as jnp
from jax import lax
from jax.experimental import pallas as pl
from jax.experimental.pallas import tpu as pltpu

F32 = jnp.float32
BF16 = jnp.bfloat16

RMS_EPS = 1e-6
ROPE_THETA = 10000.0
D_MODEL = 1024
SSD_HEAD_DIM = 64
SSD_HEADS = 16
SSD_GROUPS = 2
SSD_D_STATE = 128
SSD_D_INNER = 1024
SSD_CONV = 4
SSD_CHUNK = 128
SSD_CONV_DIM = SSD_D_INNER + 2 * SSD_GROUPS * SSD_D_STATE
DA_HEADS = 8
DA_HEAD_DIM = 64
HEADS_QK = 2 * DA_HEADS
EVEN_W_X = 1024
EVEN_W_DT = EVEN_W_X + 1536
EVEN_W_Q = EVEN_W_DT + 128
EVEN_W_K = EVEN_W_Q + 1024
EVEN_W_V = EVEN_W_K + 1024
EVEN_W_END = EVEN_W_V + 1024
DA_V_DIM = 128
DA_WIDTH = 1024
SC_WIDTH = 1024
SC_CONV = 3
D_FF = 2816
FFN_CONV = 3
FFN_CHUNKS = (768, 768, 768, 512)
LANES = 128
SUBLANES = 8
VMEM_LIMIT = 56 * 1024 * 1024
LOG2E = 1.4426950408889634
ATTN_KEY_BLOCK = 512
MAX_SAFE_SCORE_BOUND = 60.0


def _dot(a, b):
    return jnp.dot(a, b, preferred_element_type=F32)


def _split3(x):
    hi = x.astype(BF16)
    r1 = x - hi.astype(F32)
    mid = r1.astype(BF16)
    lo = (r1 - mid.astype(F32)).astype(BF16)
    return hi, mid, lo


def _dot_sel_rhs(x, sel):
    hi, mid, lo = _split3(x)
    return _dot(hi, sel) + _dot(mid, sel) + _dot(lo, sel)


def _dot_sel_rhs16(x, sel2):
    hi = x.astype(BF16)
    mid = (x - hi.astype(F32)).astype(BF16)
    return _dot(jnp.concatenate([hi, mid], axis=1), sel2)


def _dot_sel_lhs(sel, x):
    hi, mid, lo = _split3(x)
    return _dot(sel, hi) + _dot(sel, mid) + _dot(sel, lo)


def _rms(x, w):
    return x * lax.rsqrt(jnp.mean(x * x, axis=-1, keepdims=True) + RMS_EPS) * w


def _silu(x):
    return x * (1.0 / (1.0 + jnp.exp(-x)))


def _softplus(x):
    return jnp.maximum(x, 0.0) + jnp.log1p(jnp.exp(-jnp.abs(x)))


def _const_spec(shape):
    nd = len(shape)
    return pl.BlockSpec(shape, lambda *_: (0,) * nd, pipeline_mode=pl.Buffered(1))


def _layer_spec(stacked_shape, layer):
    _, rows, cols = stacked_shape
    return pl.BlockSpec((None, rows, cols), lambda *_: (layer, 0, 0), pipeline_mode=pl.Buffered(1))


def _causal_conv(buf_ref, tm, taps, first_row):
    xe = buf_ref[...]
    acc = None
    for k, w in enumerate(taps):
        back = SUBLANES - (first_row + k)
        shifted = xe if back == 0 else pltpu.roll(xe, back, 0)
        term = shifted[SUBLANES:SUBLANES + tm] * w
        acc = term if acc is None else acc + term
    return acc


def _even_in_kernel(h_ref, nw_ref, w_ref, cw_ref, cb_ref, dtb_ref, qn_ref, kn_ref, rope_ref,
                    gq_ref, gk_ref, gt_ref,
                    z_ref, xc_ref, dt_ref, q_ref, k_ref, v_ref, qsq_ref, ksq_ref, xs_ref):
    i = pl.program_id(1)
    tm = h_ref.shape[1]
    hn = _rms(h_ref[0], nw_ref[...]).astype(BF16)

    @pl.when(i == 0)
    def _():
        xs_ref[0:SUBLANES, :] = jnp.zeros((SUBLANES, SSD_CONV_DIM), F32)

    xs_ref[SUBLANES:SUBLANES + tm, :] = _dot(hn, w_ref[:, EVEN_W_X:EVEN_W_DT])
    q_raw = _dot(hn, w_ref[:, EVEN_W_Q:EVEN_W_K])

    taps = [cw_ref[k:k + 1, :] for k in range(SSD_CONV)]
    conv = _causal_conv(xs_ref, tm, taps, SUBLANES - (SSD_CONV - 1)) + cb_ref[...]
    xc_ref[0] = _silu(conv).astype(BF16)
    xs_ref[0:SUBLANES, :] = xs_ref[tm:tm + SUBLANES, :]

    k_raw = _dot(hn, w_ref[:, EVEN_W_K:EVEN_W_V])

    cos = jnp.tile(rope_ref[:, 0:LANES], (1, DA_WIDTH // LANES))
    sin_a = jnp.tile(rope_ref[:, LANES:2 * LANES], (1, DA_WIDTH // LANES))
    sin_b = jnp.tile(rope_ref[:, 2 * LANES:3 * LANES], (1, DA_WIDTH // LANES))
    half = DA_HEAD_DIM // 2

    def qk_epilogue(x, n_ref, g_ref, o_ref, sq_ref, scale):
        nsq = None
        hw = DA_WIDTH // 2
        for c0 in (0, hw):
            xh = x[:, c0:c0 + hw]
            ss = _dot((xh * xh).astype(BF16), g_ref[c0:c0 + hw, :])
            r = lax.rsqrt(ss * (1.0 / DA_HEAD_DIM) + RMS_EPS) * scale
            xn = xh * _dot_sel_rhs16(r, gt_ref[:, c0:c0 + hw]) * n_ref[:, c0:c0 + hw]
            rot = (pltpu.roll(xn, hw - half, 1) * sin_a[:, c0:c0 + hw]
                   + pltpu.roll(xn, half, 1) * sin_b[:, c0:c0 + hw])
            o_ref[0, :, c0:c0 + hw] = (xn * cos[:, c0:c0 + hw] + rot).astype(BF16)
            part = r * r * pltpu.roll(ss, LANES - HEADS_QK, 1)
            nsq = part if nsq is None else nsq + part
        nsq_t = nsq.T
        for hd in range(DA_HEADS):
            sq_ref[0, hd] = nsq_t[2 * hd:2 * hd + 2, :]

    qk_epilogue(q_raw, qn_ref, gq_ref, q_ref, qsq_ref, LOG2E * DA_HEAD_DIM ** -0.5)
    z_ref[0] = _dot(hn, w_ref[:, 0:EVEN_W_X]).astype(BF16)
    qk_epilogue(k_raw, kn_ref, gk_ref, k_ref, ksq_ref, 1.0)
    v_ref[0] = _dot(hn, w_ref[:, EVEN_W_V:EVEN_W_END]).astype(BF16)
    dt_ref[0] = _softplus(_dot(hn, w_ref[:, EVEN_W_DT:EVEN_W_Q]) + dtb_ref[...])


def _even_in(h, nw, w_in, cw, cb, dtb, qn, kn, rope, gq, gk, gt, tm):
    B, L, D = h.shape
    grid = (B, L // tm)
    row = lambda w: pl.BlockSpec((1, tm, w), lambda b, i: (b, i, 0))
    out_shape = (jax.ShapeDtypeStruct((B, L, SSD_D_INNER), BF16),
                 jax.ShapeDtypeStruct((B, L, SSD_CONV_DIM), BF16),
                 jax.ShapeDtypeStruct((B, L, LANES), F32),
                 jax.ShapeDtypeStruct((B, L, DA_WIDTH), BF16),
                 jax.ShapeDtypeStruct((B, L, DA_WIDTH), BF16),
                 jax.ShapeDtypeStruct((B, L, DA_WIDTH), BF16),
                 jax.ShapeDtypeStruct((B, DA_HEADS, 2, L), F32),
                 jax.ShapeDtypeStruct((B, DA_HEADS, 2, L), F32))
    head_rows = pl.BlockSpec((1, DA_HEADS, 2, tm), lambda b, i: (b, 0, 0, i))
    in_specs = [row(D), _const_spec(nw.shape), _const_spec(w_in.shape),
                _const_spec(cw.shape), _const_spec(cb.shape),
                _const_spec(dtb.shape), _const_spec(qn.shape), _const_spec(kn.shape),
                pl.BlockSpec((tm, 3 * LANES), lambda b, i: (i, 0)),
                _const_spec(gq.shape), _const_spec(gk.shape), _const_spec(gt.shape)]
    out_specs = (row(SSD_D_INNER), row(SSD_CONV_DIM), row(LANES), row(DA_WIDTH), row(DA_WIDTH),
                 row(DA_WIDTH), head_rows, head_rows)
    return pl.pallas_call(
        _even_in_kernel, grid=grid, in_specs=in_specs, out_specs=out_specs, out_shape=out_shape,
        scratch_shapes=[pltpu.VMEM((tm + SUBLANES, SSD_CONV_DIM), F32)],
        compiler_params=pltpu.CompilerParams(dimension_semantics=("arbitrary", "arbitrary"),
                                             vmem_limit_bytes=VMEM_LIMIT),
        name="even_in",
    )(h, nw, w_in, cw, cb, dtb, qn, kn, rope, gq, gk, gt)


def _ssd_kernel(xc_ref, dt_ref, z_ref, alog_ref, dskip_ref, nw_ref, tril_ref, expand_ref,
                y_ref, state_ref):
    T = SSD_CHUNK
    GN = SSD_GROUPS * SSD_D_STATE
    GW = SSD_D_INNER // SSD_GROUPS
    heads_per_group = SSD_HEADS // SSD_GROUPS

    @pl.when(pl.program_id(1) == 0)
    def _():
        state_ref[...] = jnp.zeros(state_ref.shape, F32)

    a = -jnp.exp(alog_ref[...])
    r_i = lax.broadcasted_iota(jnp.int32, (T, T), 0)
    c_i = lax.broadcasted_iota(jnp.int32, (T, T), 1)
    tril = r_i >= c_i
    lane = lax.broadcasted_iota(jnp.int32, (T, LANES), 1)

    for ci in range(xc_ref.shape[1] // T):
        rows = slice(ci * T, (ci + 1) * T)
        x = xc_ref[0, rows, 0:SSD_D_INNER].astype(F32)
        bm = xc_ref[0, rows, SSD_D_INNER:SSD_D_INNER + GN]
        cm = xc_ref[0, rows, SSD_D_INNER + GN:SSD_D_INNER + 2 * GN]
        dt = dt_ref[0, rows, :]
        a_cs = _dot_sel_lhs(tril_ref[...], dt * a)
        a_log2 = a_cs * LOG2E
        a_log2_t = a_log2.T
        a_last = a_cs[T - 1:T, :]

        stacked = jnp.concatenate(
            [dt, jnp.exp(a_cs), jnp.exp(a_last - a_cs),
             jnp.broadcast_to(jnp.exp(a_last), (SUBLANES, LANES))], axis=0)
        ex = _dot_sel_rhs16(stacked, expand_ref[...])
        dt_e, expa_e, decay_e = ex[0:T], ex[T:2 * T], ex[2 * T:3 * T]
        chunk_decay = ex[3 * T:3 * T + 1]
        xdt = x * dt_e

        xdt_b = xdt.astype(BF16)
        y_parts = []
        for g in range(SSD_GROUPS):
            cg = cm[:, g * SSD_D_STATE:(g + 1) * SSD_D_STATE]
            bg = bm[:, g * SSD_D_STATE:(g + 1) * SSD_D_STATE]
            cb = lax.dot_general(cg, bg, (((1,), (1,)), ((), ())), preferred_element_type=F32)
            for hp in range(heads_per_group // 2):
                h0 = g * heads_per_group + 2 * hp
                ms = []
                for h in (h0, h0 + 1):
                    seg = a_log2[:, h:h + 1] - a_log2_t[h:h + 1, :]
                    ms.append((cb * jnp.exp2(jnp.where(tril, seg, -jnp.inf))).astype(BF16))
                lhs = jnp.concatenate(ms, axis=1)
                xp = xdt_b[:, h0 * SSD_HEAD_DIM:(h0 + 2) * SSD_HEAD_DIM]
                rhs = jnp.concatenate([jnp.where(lane < SSD_HEAD_DIM, xp, 0),
                                       jnp.where(lane >= SSD_HEAD_DIM, xp, 0)], axis=0)
                y_parts.append(_dot(lhs, rhs))
        y_diag = jnp.concatenate(y_parts, axis=1)

        xdec = (xdt * decay_e).astype(BF16)
        y_off_parts = []
        for g in range(SSD_GROUPS):
            cg = cm[:, g * SSD_D_STATE:(g + 1) * SSD_D_STATE]
            y_off_parts.append(_dot(cg, state_ref[g].astype(BF16)))
            bg_t = bm[:, g * SSD_D_STATE:(g + 1) * SSD_D_STATE].astype(F32).T.astype(BF16)
            upd = _dot(bg_t, xdec[:, g * GW:(g + 1) * GW])
            state_ref[g] = state_ref[g] * chunk_decay[:, g * GW:(g + 1) * GW] + upd
        y_off = jnp.concatenate(y_off_parts, axis=1) * expa_e

        y = y_diag + y_off + dskip_ref[...] * x
        y = y * _silu(z_ref[0, rows, :].astype(F32))
        outs = []
        for g in range(SSD_GROUPS):
            yg = y[:, g * GW:(g + 1) * GW]
            outs.append(yg * lax.rsqrt(jnp.mean(yg * yg, axis=-1, keepdims=True) + RMS_EPS))
        y_ref[0, rows, :] = (jnp.concatenate(outs, axis=1) * nw_ref[...]).astype(BF16)


def _ssd(xc, dtf, z, alog, dskip, nw, ts):
    B, L, _ = xc.shape
    T = SSD_CHUNK
    grid = (B, L // ts)
    row = lambda w: pl.BlockSpec((1, ts, w), lambda b, c: (b, c, 0))
    tril = (jnp.arange(T)[:, None] >= jnp.arange(T)[None, :]).astype(BF16)
    expand = (jnp.arange(SSD_D_INNER)[None, :] // SSD_HEAD_DIM
              == jnp.arange(2 * LANES)[:, None] % LANES).astype(BF16)
    return pl.pallas_call(
        _ssd_kernel, grid=grid,
        in_specs=[row(SSD_CONV_DIM), row(LANES), row(SSD_D_INNER),
                  _const_spec(alog.shape), _const_spec(dskip.shape), _const_spec(nw.shape),
                  _const_spec(tril.shape), _const_spec(expand.shape)],
        out_specs=row(SSD_D_INNER),
        out_shape=jax.ShapeDtypeStruct((B, L, SSD_D_INNER), BF16),
        scratch_shapes=[pltpu.VMEM((SSD_GROUPS, SSD_D_STATE, SSD_D_INNER // SSD_GROUPS), F32)],
        compiler_params=pltpu.CompilerParams(dimension_semantics=("arbitrary", "arbitrary"),
                                             vmem_limit_bytes=VMEM_LIMIT),
        name="ssd",
    )(xc, dtf, z, alog, dskip, nw, tril, expand)


def _attn_kernel(q_ref, k_ref, v_ref, qsq_ref, ksq_ref, lq1_ref, lk1_ref, lq2_ref, lk2_ref, sub_ref,
                 o_ref, q2_ref, *, tq, lambda_init):
    i = pl.program_id(2)
    tk = ATTN_KEY_BLOCK
    nq = tq // tk
    q = q_ref[0]
    lane = lax.broadcasted_iota(jnp.int32, q.shape, 1)
    zero = jnp.zeros_like(q)
    q_map1 = jnp.where(lane < DA_HEAD_DIM, q, zero)
    q_map2 = jnp.where(lane >= DA_HEAD_DIM, q, zero)
    for g in range(nq):
        q2_ref[2 * g * tk:(2 * g + 1) * tk, :] = q_map1[g * tk:(g + 1) * tk]
        q2_ref[(2 * g + 1) * tk:(2 * g + 2) * tk, :] = q_map2[g * tk:(g + 1) * tk]

    def from_group(g):
        return slice(2 * g * tk, 2 * tq)

    def scores(j, first_group=0):
        kb = k_ref[0, pl.ds(pl.multiple_of(j * tk, tk), tk), :]
        return lax.dot_general(kb, q2_ref[from_group(first_group), :], (((1,), (1,)), ((), ())),
                               preferred_element_type=F32)

    def causal(s):
        kpos = lax.broadcasted_iota(jnp.int32, s.shape, 0)
        col = lax.broadcasted_iota(jnp.int32, s.shape, 1)
        qpos = jnp.where(col >= tk, col - tk, col)
        return jnp.where(kpos <= qpos, s, -jnp.inf)

    def diagonal_scores(g):
        s = scores(nq * i + g, first_group=g)
        if g + 1 == nq:
            return causal(s)
        return jnp.concatenate([causal(s[:, 0:2 * tk]), s[:, 2 * tk:]], axis=1)

    def merge(full, part, g):
        return part if g == 0 else jnp.concatenate([full[:, 0:2 * g * tk], part], axis=1)

    def v_block(j):
        return v_ref[0, pl.ds(pl.multiple_of(j * tk, tk), tk), :]

    def update(j, s, carry):
        m, l, acc = carry
        m_new = jnp.maximum(m, jnp.max(s, axis=0, keepdims=True))
        p = jnp.exp2(s - m_new)
        alpha = jnp.exp2(m - m_new)
        l_new = alpha * l + jnp.sum(p, axis=0, keepdims=True)
        pv = lax.dot_general(v_block(j), p.astype(BF16), (((0,), (0,)), ((), ())),
                             preferred_element_type=F32)
        return m_new, l_new, alpha * acc + pv

    def pair(j0, carry):
        s_a = scores(j0)
        s_b = scores(j0 + 1)
        return update(j0 + 1, s_b, update(j0, s_a, carry))

    def online_path():
        init = (jnp.full((1, 2 * tq), -jnp.inf, F32), jnp.zeros((1, 2 * tq), F32),
                jnp.zeros((DA_V_DIM, 2 * tq), F32))
        m, l, acc = lax.fori_loop(0, (nq // 2) * i, lambda t, c: pair(2 * t, c), init)
        for g in range(nq):
            cols = from_group(g)
            part = update(nq * i + g, diagonal_scores(g), (m[:, cols], l[:, cols], acc[:, cols]))
            m, l, acc = merge(m, part[0], g), merge(l, part[1], g), merge(acc, part[2], g)
        return l, acc

    qsq = qsq_ref[0, 0]
    kmax = jnp.max(ksq_ref[0, 0], axis=1, keepdims=True)
    bound = jnp.sqrt(jnp.concatenate(
        [qsq[mp:mp + 1, g * tk:(g + 1) * tk] * kmax[mp:mp + 1]
         for g in range(nq) for mp in range(2)], axis=1))

    def bounded_update(j, s, shift, carry):
        l, acc = carry
        p = jnp.exp2(s - shift)
        pv = lax.dot_general(v_block(j), p.astype(BF16), (((0,), (0,)), ((), ())),
                             preferred_element_type=F32)
        return l + jnp.sum(p, axis=0, keepdims=True), acc + pv

    def bounded_pair(j0, carry):
        carry = bounded_update(j0, scores(j0), bound, carry)
        return bounded_update(j0 + 1, scores(j0 + 1), bound, carry)

    def bounded_path():
        init = (jnp.zeros((1, 2 * tq), F32), jnp.zeros((DA_V_DIM, 2 * tq), F32))
        l, acc = lax.fori_loop(0, (nq // 2) * i, lambda t, c: bounded_pair(2 * t, c), init)
        for g in range(nq):
            cols = from_group(g)
            part = bounded_update(nq * i + g, diagonal_scores(g), bound[:, cols],
                                  (l[:, cols], acc[:, cols]))
            l, acc = merge(l, part[0], g), merge(acc, part[1], g)
        return l, acc

    l, acc = lax.cond(jnp.max(bound) <= MAX_SAFE_SCORE_BOUND, bounded_path, online_path)
    o_t = acc * (1.0 / l)

    lam = (jnp.exp(jnp.sum(lq1_ref[...] * lk1_ref[...], axis=-1, keepdims=True))
           - jnp.exp(jnp.sum(lq2_ref[...] * lk2_ref[...], axis=-1, keepdims=True)) + lambda_init)
    o_map1 = jnp.concatenate([o_t[:, 2 * g * tk:(2 * g + 1) * tk] for g in range(nq)], axis=1)
    o_map2 = jnp.concatenate([o_t[:, (2 * g + 1) * tk:(2 * g + 2) * tk] for g in range(nq)], axis=1)
    d = o_map1 - lam * o_map2
    scale = lax.rsqrt(jnp.mean(d * d, axis=0, keepdims=True) + RMS_EPS) * (1.0 - lambda_init)
    o_ref[0] = ((d * scale).T * sub_ref[...]).astype(BF16)


def _diff_attn(q, k, v, q_sq, k_sq, lq1, lk1, lq2, lk2, sub, lambda_init, tq):
    B, L, _ = q.shape
    grid = (B, DA_HEADS, L // tq)
    full = pl.BlockSpec((1, L, DA_V_DIM), lambda b, h, i: (b, 0, h))
    blk = pl.BlockSpec((1, tq, DA_V_DIM), lambda b, h, i: (b, i, h))
    return pl.pallas_call(
        functools.partial(_attn_kernel, tq=tq, lambda_init=lambda_init),
        grid=grid,
        scratch_shapes=[pltpu.VMEM((2 * tq, DA_V_DIM), BF16)],
        in_specs=[blk, full, full,
                  pl.BlockSpec((1, 1, 2, tq), lambda b, h, i: (b, h, 0, i)),
                  pl.BlockSpec((1, 1, 2, L), lambda b, h, i: (b, h, 0, 0)),
                  _const_spec(lq1.shape), _const_spec(lk1.shape),
                  _const_spec(lq2.shape), _const_spec(lk2.shape), _const_spec(sub.shape)],
        out_specs=blk,
        out_shape=jax.ShapeDtypeStruct((B, L, DA_WIDTH), BF16),
        compiler_params=pltpu.CompilerParams(
            dimension_semantics=("arbitrary", "arbitrary", "arbitrary"),
            vmem_limit_bytes=VMEM_LIMIT),
        name="diff_attn",
    )(q, k, v, q_sq, k_sq, lq1, lk1, lq2, lk2, sub)


def _ffn_tile(o_ref, i, nw_ref, wup_ref, cw_ref, cb_ref, wdn_ref, buf_ref, tm):
    hn = _rms(o_ref[0], nw_ref[...]).astype(BF16)
    lead = SUBLANES - (FFN_CONV - 1)

    @pl.when(i == 0)
    def _():
        buf_ref[:, 0:SUBLANES, :] = jnp.zeros((buf_ref.shape[0], SUBLANES, buf_ref.shape[2]), F32)

    starts = [sum(FFN_CHUNKS[:c]) for c in range(len(FFN_CHUNKS))]

    def slot(c, part):
        return buf_ref.at[2 * c + part, :, 0:FFN_CHUNKS[c]]

    def up_project(c):
        for part in range(2):
            col = part * D_FF + starts[c]
            slot(c, part)[SUBLANES:SUBLANES + tm, :] = _dot(hn, wup_ref[:, col:col + FFN_CHUNKS[c]])

    up_project(0)
    for c, width in enumerate(FFN_CHUNKS):
        if c + 1 < len(FFN_CHUNKS):
            up_project(c + 1)
        acts = []
        for part in range(2):
            col = part * D_FF + starts[c]
            taps = [cw_ref[k:k + 1, col:col + width] for k in range(FFN_CONV)]
            acts.append(_causal_conv(slot(c, part), tm, taps, lead) + cb_ref[:, col:col + width])
            slot(c, part)[0:SUBLANES, :] = slot(c, part)[tm:tm + SUBLANES, :]
        act = (_silu(acts[0]) * acts[1]).astype(BF16)
        o_ref[0] += _dot(act, wdn_ref[starts[c]:starts[c] + width, :])


def _out_ffn_kernel(h_ref, ys_ref, ya_ref, wo_ref, nw_ref, wup_ref, cw_ref, cb_ref, wdn_ref,
                    o_ref, buf_ref):
    i = pl.program_id(1)
    tm = h_ref.shape[1]
    o_ref[0] = (h_ref[0] + _dot(ys_ref[0], wo_ref[0:SSD_D_INNER, :])
                + _dot(ya_ref[0], wo_ref[SSD_D_INNER:SSD_D_INNER + DA_WIDTH, :]))
    _ffn_tile(o_ref, i, nw_ref, wup_ref, cw_ref, cb_ref, wdn_ref, buf_ref, tm)


def _out_ffn(h, ys, ya, wo, nw, wup, cw, cb, wdn, layer, tm):
    B, L, D = h.shape
    grid = (B, L // tm)
    row = lambda w: pl.BlockSpec((1, tm, w), lambda b, i: (b, i, 0))
    return pl.pallas_call(
        _out_ffn_kernel, grid=grid,
        in_specs=[row(D), row(SSD_D_INNER), row(DA_WIDTH), _const_spec(wo.shape),
                  _const_spec(nw.shape), _layer_spec(wup.shape, layer), _const_spec(cw.shape),
                  _const_spec(cb.shape), _layer_spec(wdn.shape, layer)],
        out_specs=row(D),
        out_shape=jax.ShapeDtypeStruct((B, L, D), F32),
        scratch_shapes=[pltpu.VMEM((2 * len(FFN_CHUNKS), tm + SUBLANES, max(FFN_CHUNKS)), F32)],
        compiler_params=pltpu.CompilerParams(dimension_semantics=("arbitrary", "arbitrary"),
                                             vmem_limit_bytes=VMEM_LIMIT),
        name="out_ffn",
    )(h, ys, ya, wo, nw, wup, cw, cb, wdn)


def _sc_ffn_kernel(h_ref, mnw_ref, win_ref, scw_ref, wout_ref, nw_ref, wup_ref, cw_ref, cb_ref,
                   wdn_ref, o_ref, mbuf_ref, buf_ref):
    i = pl.program_id(1)
    tm = h_ref.shape[1]
    h = h_ref[0]
    hn = _rms(h, mnw_ref[...]).astype(BF16)

    @pl.when(i == 0)
    def _():
        mbuf_ref[0:SUBLANES, :] = jnp.zeros((SUBLANES, SC_WIDTH), F32)

    cg = _dot(hn, win_ref[:, SC_WIDTH:2 * SC_WIDTH])
    u = _dot(hn, win_ref[:, 2 * SC_WIDTH:3 * SC_WIDTH])
    bg = _dot(hn, win_ref[:, 0:SC_WIDTH])
    mbuf_ref[SUBLANES:SUBLANES + tm, :] = cg * u
    taps = [scw_ref[k:k + 1, :] for k in range(SC_CONV)]
    conv = _causal_conv(mbuf_ref, tm, taps, SUBLANES - (SC_CONV - 1))
    mbuf_ref[0:SUBLANES, :] = mbuf_ref[tm:tm + SUBLANES, :]
    o_ref[0] = h + _dot((bg * conv).astype(BF16), wout_ref[...])
    _ffn_tile(o_ref, i, nw_ref, wup_ref, cw_ref, cb_ref, wdn_ref, buf_ref, tm)


def _sc_ffn(h, mnw, win, scw, wout, nw, wup, cw, cb, wdn, layer, tm):
    B, L, D = h.shape
    grid = (B, L // tm)
    row = pl.BlockSpec((1, tm, D), lambda b, i: (b, i, 0))
    return pl.pallas_call(
        _sc_ffn_kernel, grid=grid,
        in_specs=[row, _const_spec(mnw.shape), _const_spec(win.shape), _const_spec(scw.shape),
                  _const_spec(wout.shape), _const_spec(nw.shape), _layer_spec(wup.shape, layer),
                  _const_spec(cw.shape), _const_spec(cb.shape), _layer_spec(wdn.shape, layer)],
        out_specs=row,
        out_shape=jax.ShapeDtypeStruct((B, L, D), F32),
        scratch_shapes=[pltpu.VMEM((tm + SUBLANES, SC_WIDTH), F32),
                        pltpu.VMEM((2 * len(FFN_CHUNKS), tm + SUBLANES, max(FFN_CHUNKS)), F32)],
        compiler_params=pltpu.CompilerParams(dimension_semantics=("arbitrary", "arbitrary"),
                                             vmem_limit_bytes=VMEM_LIMIT),
        name="sc_ffn",
    )(h, mnw, win, scw, wout, nw, wup, cw, cb, wdn)


def _rope_table(seq):
    inv = 1.0 / (ROPE_THETA ** (jnp.arange(0, DA_HEAD_DIM, 2, dtype=F32) / DA_HEAD_DIM))
    ang = jnp.arange(seq, dtype=F32)[:, None] * inv[None, :]
    ang = jnp.concatenate([ang, ang], axis=-1)
    cos, sin = jnp.cos(ang), jnp.sin(ang)
    first = jnp.arange(DA_HEAD_DIM) < DA_HEAD_DIM // 2
    sin_a = jnp.where(first, -sin, 0.0)
    sin_b = jnp.where(first, 0.0, sin)
    rep = LANES // DA_HEAD_DIM
    return jnp.concatenate([jnp.tile(cos, (1, rep)), jnp.tile(sin_a, (1, rep)),
                            jnp.tile(sin_b, (1, rep))], axis=-1)


def _row(v):
    return v.reshape(1, -1).astype(F32)


def _pad_lanes(v):
    return jnp.pad(v.reshape(1, -1).astype(F32), ((0, 0), (0, LANES - v.shape[-1])))


def kernel(x, mix_norm, ffn_norm, hy_w_in, hy_conv_w, hy_conv_b, hy_dt_bias, hy_a_log, hy_d_skip,
           hy_ssd_norm, hy_q_norm, hy_k_norm, hy_lambda_q1, hy_lambda_k1, hy_lambda_q2,
           hy_lambda_k2, hy_subln, hy_w_out, sc_w_in, sc_conv_w, sc_w_out, ffn_w_up, ffn_conv_w,
           ffn_conv_b, ffn_w_down):
    B, L, D = x.shape
    tm = min(512, L)
    tq = min(2048, L)

    w = hy_w_in[0]
    o_x = SSD_D_INNER
    o_dt = o_x + SSD_CONV_DIM
    o_q = o_dt + SSD_HEADS
    o_k = o_q + DA_WIDTH
    o_v = o_k + DA_WIDTH
    w_in = jnp.concatenate(
        [w[:, 0:o_dt].astype(BF16),
         jnp.pad(w[:, o_dt:o_q], ((0, 0), (0, LANES - SSD_HEADS))).astype(BF16),
         w[:, o_q:o_v + DA_WIDTH].astype(BF16)], axis=1)
    head_of = jnp.arange(DA_WIDTH) // DA_HEAD_DIM
    g = (head_of[:, None] == jnp.arange(LANES)[None, :]).astype(F32)
    gt = jnp.concatenate([g.T, g.T], axis=0).astype(BF16)
    q_gain = jnp.tile(hy_q_norm[0], DA_WIDTH // DA_HEAD_DIM).astype(F32)
    k_gain = jnp.tile(hy_k_norm[0], DA_WIDTH // DA_HEAD_DIM).astype(F32)

    def head_sum_selector(gain):
        weighted = jnp.roll(g * (gain * gain)[:, None], HEADS_QK, axis=1)
        return (g + weighted).astype(BF16)

    rope = _rope_table(L)
    z, xc, dtf, q, k, v, q_sq, k_sq = _even_in(
        x, _row(mix_norm[0]), w_in, hy_conv_w[0].astype(F32),
        _row(hy_conv_b[0]), _pad_lanes(hy_dt_bias[0]), _row(q_gain), _row(k_gain), rope,
        head_sum_selector(q_gain), head_sum_selector(k_gain), gt, tm)
    y_ssd = _ssd(xc, dtf, z, _pad_lanes(hy_a_log[0]),
                 _row(jnp.repeat(hy_d_skip[0], SSD_HEAD_DIM)), _row(hy_ssd_norm[0]), tm)
    lambda_init = 0.8 - 0.6 * math.exp(-0.3 * 0)
    y_att = _diff_attn(q, k, v, q_sq, k_sq,
                       _row(hy_lambda_q1[0]), _row(hy_lambda_k1[0]),
                       _row(hy_lambda_q2[0]), _row(hy_lambda_k2[0]), _row(hy_subln[0]),
                       lambda_init, tq)
    w_up = ffn_w_up.astype(BF16)
    w_down = ffn_w_down.astype(BF16)
    h = _out_ffn(x, y_ssd, y_att, hy_w_out[0].astype(BF16), _row(ffn_norm[0]),
                 w_up, ffn_conv_w[0].astype(F32), _row(ffn_conv_b[0]), w_down, 0, tm)

    h = _sc_ffn(h, _row(mix_norm[1]), sc_w_in[0].astype(BF16), sc_conv_w[0].astype(F32),
                sc_w_out[0].astype(BF16), _row(ffn_norm[1]), w_up,
                ffn_conv_w[1].astype(F32), _row(ffn_conv_b[1]), w_down, 1, tm)
    return h
```
